```python
import jax, jax.numpy as jnp
from jax import lax
import numpy as np

D_MODEL = 1024
BATCH = 4
SEQ = 4096
DEPTH = 1
DEC_BATCH = 128
DEC_SEQ = 8
PAST_LEN = 2048
PAGE_SIZE = 128

RET_HEADS = 4
RET_DK = 128
RET_DV = 256
FOX_HEADS = 8
FOX_DH = 128
D_FF = 4 * D_MODEL
CHUNK = 128
Q_BLOCK = 128
ROPE_BASE = 10000.0
LN_EPS = 1e-5
GN_EPS = 1e-6
FORGET_BIAS_INIT = 4.0
ALPHA = (2.0 * DEPTH) ** 0.25
BETA = (8.0 * DEPTH) ** -0.25
RET_QK_W = RET_HEADS * RET_DK
RET_V_W = RET_HEADS * RET_DV
FOX_W = FOX_HEADS * FOX_DH
SPLIT_WIDTHS = (RET_QK_W, RET_QK_W, RET_V_W, RET_V_W, FOX_W, FOX_W, FOX_W, FOX_HEADS, D_MODEL, D_MODEL)
D_IN = sum(SPLIT_WIDTHS)

kernel_name = 'retnet_fox_gated_hybrid_step'


def layer_norm(x, g, b):
    xf = x.astype(jnp.float32)
    mu = jnp.mean(xf, -1, keepdims=True)
    var = jnp.mean(jnp.square(xf - mu), -1, keepdims=True)
    return ((xf - mu) * lax.rsqrt(var + LN_EPS) * g + b).astype(x.dtype)


def head_norm(x, g):
    xf = x.astype(jnp.float32)
    mu = jnp.mean(xf, -1, keepdims=True)
    var = jnp.mean(jnp.square(xf - mu), -1, keepdims=True)
    y = (xf - mu) * lax.rsqrt(var + GN_EPS)
    B, T, H, Dv = x.shape
    return (y.reshape(B, T, H * Dv) * g).astype(x.dtype)


def rope(x, pos):
    D = x.shape[-1]
    half = D // 2
    inv_freq = ROPE_BASE ** (-jnp.arange(half, dtype=jnp.float32) / half)
    ang = pos.astype(jnp.float32)[:, None] * inv_freq[None, :]
    c = jnp.cos(ang)[None, :, None, :]
    s = jnp.sin(ang)[None, :, None, :]
    xf = x.astype(jnp.float32)
    x1, x2 = xf[..., :half], xf[..., half:]
    return jnp.concatenate([x1 * c - x2 * s, x1 * s + x2 * c], -1).astype(x.dtype)


def retention_chunked(q, k, v, S0):
    B, T, H, Dk = q.shape
    Dv = v.shape[-1]
    C = min(CHUNK, T)
    n = T // C
    ld = jnp.log(1.0 - 2.0 ** (-5.0 - jnp.arange(H, dtype=jnp.float32)))
    i = jnp.arange(C)
    causal = i[:, None] >= i[None, :]
    expo = jnp.where(causal, (i[:, None] - i[None, :]).astype(jnp.float32), 0.0)[None] * ld[:, None, None]
    dmat = jnp.where(causal[None], jnp.exp(expo), 0.0)
    cross_decay = jnp.exp((i + 1).astype(jnp.float32)[:, None] * ld[None, :])
    k_decay = jnp.exp((C - 1 - i).astype(jnp.float32)[:, None] * ld[None, :])
    chunk_decay = jnp.exp(C * ld)

    def to_chunks(a):
        return jnp.moveaxis(a.astype(jnp.float32).reshape(B, n, C, H, a.shape[-1]), 1, 0)

    def step(S, xs):
        qc, kc, vc = xs
        sc = jnp.einsum('bihd,bjhd->bhij', qc, kc) * dmat[None]
        intra = jnp.einsum('bhij,bjhv->bihv', sc, vc)
        cross = jnp.einsum('bihd,bhdv->bihv', qc, S) * cross_decay[None, :, :, None]
        S_new = S * chunk_decay[None, :, None, None] + jnp.einsum('bjhd,bjhv->bhdv', kc * k_decay[None, :, :, None], vc)
        return S_new, intra + cross

    S_fin, o = lax.scan(step, S0.astype(jnp.float32), (to_chunks(q), to_chunks(k), to_chunks(v)))
    o = jnp.moveaxis(o, 0, 1).reshape(B, T, H, Dv)
    return o.astype(q.dtype), S_fin


def fox_attend(q, Fq, k, v, Gk, mask):
    scale = q.shape[-1] ** -0.5
    s = jnp.einsum('bqhd,bkhd->bhqk', q, k).astype(jnp.float32) * scale + (Fq - Gk)
    s = jnp.where(mask[None, None], s, -jnp.inf)
    p = jax.nn.softmax(s, axis=-1)
    return jnp.einsum('bhqk,bkhd->bqhd', p.astype(v.dtype), v)


def fox_prompt(q, k, v, logf):
    B, T, H, D = q.shape
    F = jnp.cumsum(logf.astype(jnp.float32), axis=1)
    Gk = jnp.transpose(F, (0, 2, 1))[:, :, None, :]
    nb = T // Q_BLOCK
    qb = jnp.swapaxes(q.reshape(B, nb, Q_BLOCK, H, D), 0, 1)
    Fb = jnp.swapaxes(F.reshape(B, nb, Q_BLOCK, H), 0, 1)
    kpos = jnp.arange(T)

    def block(args):
        qi, Fi, bi = args
        qpos = bi * Q_BLOCK + jnp.arange(Q_BLOCK)
        mask = kpos[None, :] <= qpos[:, None]
        return fox_attend(qi, jnp.transpose(Fi, (0, 2, 1))[..., None], k, v, Gk, mask)

    out = lax.map(block, (qb, Fb, jnp.arange(nb)))
    return jnp.swapaxes(out, 0, 1).reshape(B, T, H, D)


def fox_sample(q, k_new, v_new, logf_new, k_past, v_past, logf_past):
    T = q.shape[1]
    P = k_past.shape[1]
    Fn = jnp.cumsum(logf_new.astype(jnp.float32), axis=1)
    lp = logf_past.astype(jnp.float32)
    R = jnp.sum(lp, axis=1, keepdims=True) - jnp.cumsum(lp, axis=1)
    G = jnp.concatenate([-R, Fn], axis=1)
    Gk = jnp.transpose(G, (0, 2, 1))[:, :, None, :]
    Fq = jnp.transpose(Fn, (0, 2, 1))[..., None]
    k = jnp.concatenate([k_past.astype(k_new.dtype), k_new], axis=1)
    v = jnp.concatenate([v_past.astype(v_new.dtype), v_new], axis=1)
    mask = jnp.concatenate([jnp.ones((T, P), bool), jnp.tril(jnp.ones((T, T), bool))], axis=1)
    return fox_attend(q, Fq, k, v, Gk, mask)


def layer(x, pos, S0, past, p):
    (w_in, b_forget, ret_gn_gain, w_ret_proj, w_fox_proj, w_out,
     ln1_g, ln1_b, w_ff_up, w_ff_down, ln2_g, ln2_b) = p
    B, T, _ = x.shape
    proj = jnp.einsum('btd,de->bte', x, w_in)
    split_idx = [int(s) for s in np.cumsum(SPLIT_WIDTHS)[:-1]]
    rq, rk, rv, rg, fq, fk, fv, ff, ga, gb = jnp.split(proj, split_idx, axis=-1)
    rq = rope(rq.reshape(B, T, RET_HEADS, RET_DK), pos)
    rk = rope(rk.reshape(B, T, RET_HEADS, RET_DK), pos) * (RET_DK ** -0.5)
    rv = rv.reshape(B, T, RET_HEADS, RET_DV)
    ret_o, S_new = retention_chunked(rq, rk, rv, S0)
    ret_o = head_norm(ret_o, ret_gn_gain) * jax.nn.silu(rg)
    fq = fq.reshape(B, T, FOX_HEADS, FOX_DH)
    fk = fk.reshape(B, T, FOX_HEADS, FOX_DH)
    fv = fv.reshape(B, T, FOX_HEADS, FOX_DH)
    logf = jax.nn.log_sigmoid(ff.astype(jnp.float32) + b_forget.astype(jnp.float32))
    if past is None:
        fox_o = fox_prompt(fq, fk, fv, logf)
    else:
        fox_o = fox_sample(fq, fk, fv, logf, *past)
    fox_o = fox_o.reshape(B, T, FOX_W)
    merged = (jax.nn.sigmoid(ga) * jnp.einsum('btc,cd->btd', ret_o, w_ret_proj)
              + jax.nn.sigmoid(gb) * jnp.einsum('btc,cd->btd', fox_o, w_fox_proj))
    mix = jnp.einsum('btd,de->bte', merged, w_out)
    h = layer_norm(ALPHA * x + mix, ln1_g, ln1_b)
    u = jnp.square(jax.nn.relu(jnp.einsum('btd,df->btf', h, w_ff_up)))
    y = layer_norm(ALPHA * h + jnp.einsum('btf,fd->btd', u, w_ff_down), ln2_g, ln2_b)
    return y, (fk, fv, logf, S_new)


def setup_inputs(seed: int = 0) -> dict:
    key = jax.random.key(seed)
    ks = jax.random.split(key, 20)
    f32 = jnp.float32
    n_pages = PAST_LEN // PAGE_SIZE
    n_used = DEC_BATCH * n_pages
    n_phys = n_used + n_used // 4

    def nrm(k, shape, s):
        return jax.random.normal(k, shape, f32) * s

    page_table = jax.random.permutation(ks[0], n_phys)[:n_used].reshape(DEC_BATCH, n_pages).astype(jnp.int32)
    return {
        'x_prompt': nrm(ks[1], (BATCH, SEQ, D_MODEL), 1.0),
        'x_sample': nrm(ks[2], (DEC_BATCH, DEC_SEQ, D_MODEL), 1.0),
        'cache_k': nrm(ks[3], (DEPTH, n_phys, PAGE_SIZE, FOX_HEADS, FOX_DH), 1.0),
        'cache_v': nrm(ks[4], (DEPTH, n_phys, PAGE_SIZE, FOX_HEADS, FOX_DH), 1.0),
        'cache_logf': jax.nn.log_sigmoid(FORGET_BIAS_INIT + nrm(ks[5], (DEPTH, n_phys, PAGE_SIZE, FOX_HEADS), 1.0)),
        'state_ret': nrm(ks[6], (DEPTH, DEC_BATCH, RET_HEADS, RET_DK, RET_DV), 0.1),
        'page_table': page_table,
        'w_in': nrm(ks[7], (DEPTH, D_MODEL, D_IN), D_MODEL ** -0.5),
        'b_forget': FORGET_BIAS_INIT + nrm(ks[8], (DEPTH, FOX_HEADS), 0.1),
        'ret_gn_gain': 1.0 + nrm(ks[9], (DEPTH, RET_V_W), 0.02),
        'w_ret_proj': nrm(ks[10], (DEPTH, RET_V_W, D_MODEL), BETA * RET_V_W ** -0.5),
        'w_fox_proj': nrm(ks[11], (DEPTH, FOX_W, D_MODEL), BETA * FOX_W ** -0.5),
        'w_out': nrm(ks[12], (DEPTH, D_MODEL, D_MODEL), BETA * D_MODEL ** -0.5),
        'ln1_g': 1.0 + nrm(ks[13], (DEPTH, D_MODEL), 0.02),
        'ln1_b': nrm(ks[14], (DEPTH, D_MODEL), 0.02),
        'w_ff_up': nrm(ks[15], (DEPTH, D_MODEL, D_FF), BETA * D_MODEL ** -0.5),
        'w_ff_down': nrm(ks[16], (DEPTH, D_FF, D_MODEL), BETA * D_FF ** -0.5),
        'ln2_g': 1.0 + nrm(ks[17], (DEPTH, D_MODEL), 0.02),
        'ln2_b': nrm(ks[18], (DEPTH, D_MODEL), 0.02),
    }


def reference(x_prompt, x_sample, cache_k, cache_v, cache_logf, state_ret, page_table,
              w_in, b_forget, ret_gn_gain, w_ret_proj, w_fox_proj, w_out,
              ln1_g, ln1_b, w_ff_up, w_ff_down, ln2_g, ln2_b):
    B, T = x_prompt.shape[:2]
    DB, TS = x_sample.shape[:2]
    past_len = page_table.shape[1] * cache_k.shape[2]
    pos_p = jnp.arange(T, dtype=jnp.int32)
    pos_s = past_len + jnp.arange(TS, dtype=jnp.int32)
    yp, ys = x_prompt, x_sample
    st_p, st_s = [], []
    for l in range(DEPTH):
        params = (w_in[l], b_forget[l], ret_gn_gain[l], w_ret_proj[l], w_fox_proj[l], w_out[l],
                  ln1_g[l], ln1_b[l], w_ff_up[l], w_ff_down[l], ln2_g[l], ln2_b[l])
        S0 = jnp.zeros((B, RET_HEADS, RET_DK, RET_DV), jnp.float32)
        yp, sp = layer(yp, pos_p, S0, None, params)
        past = (cache_k[l][page_table].reshape(DB, past_len, FOX_HEADS, FOX_DH),
                cache_v[l][page_table].reshape(DB, past_len, FOX_HEADS, FOX_DH),
                cache_logf[l][page_table].reshape(DB, past_len, FOX_HEADS))
        ys, ss = layer(ys, pos_s, state_ret[l], past, params)
        st_p.append(sp)
        st_s.append(ss)
    k_prompt = jnp.stack([s[0] for s in st_p])
    v_prompt = jnp.stack([s[1] for s in st_p])
    logf_prompt = jnp.stack([s[2] for s in st_p])
    ret_prompt = jnp.stack([s[3] for s in st_p])
    k_sample = jnp.stack([s[0] for s in st_s])
    v_sample = jnp.stack([s[1] for s in st_s])
    logf_sample = jnp.stack([s[2] for s in st_s])
    ret_sample = jnp.stack([s[3] for s in st_s])
    return (yp, ys, k_prompt, v_prompt, logf_prompt, ret_prompt, k_sample, v_sample, logf_sample, ret_sample)
```

```python
import functools
import math

import jax
import jax.numpy as jnp
from jax import lax
from jax.experimental import pallas as pl
from jax.experimental.pallas import tpu as pltpu

F32 = jnp.float32
BF16 = jnp.bfloat16
HIGHEST = lax.Precision.HIGHEST

D_MODEL = 1024
RET_HEADS = 4
RET_DK = 128
RET_DV = 256
FOX_HEADS = 8
FOX_DH = 128
D_FF = 4 * D_MODEL
CHUNK = 128
ROPE_BASE = 10000.0
LN_EPS = 1e-5
GN_EPS = 1e-6
DEPTH = 1
ALPHA = (2.0 * DEPTH) ** 0.25
RET_QK_W = RET_HEADS * RET_DK
RET_V_W = RET_HEADS * RET_DV
FOX_W = FOX_HEADS * FOX_DH
LANES = 128
SUBLANES = 8

NT_DIMS = (((1,), (1,)), ((), ()))
TN_DIMS = (((0,), (0,)), ((), ()))

C_RQ, C_RK, C_RV, C_RG, C_FQ, C_FK, C_FV, C_GA, C_GB, C_FF, C_END = (
    0, 512, 1024, 2048, 3072, 4096, 5120, 6144, 7168, 8192, 8320)


def _resident(shape):
    return pl.BlockSpec(shape, lambda *_: (0,) * len(shape), pipeline_mode=pl.Buffered(1))


def _log_sigmoid(z):
    return jnp.minimum(z, 0.0) - jnp.log1p(jnp.exp(-jnp.abs(z)))


def _layer_norm(z, g, b):
    mu = jnp.mean(z, axis=-1, keepdims=True)
    zc = z - mu
    var = jnp.mean(zc * zc, axis=-1, keepdims=True)
    return zc * lax.rsqrt(var + LN_EPS) * g + b


def _ret_log_decay(h):
    return math.log(1.0 - 2.0 ** (-5.0 - h))


def _inproj_kernel(x_ref, w_ref, cos_ref, sin_ref, bf_ref,
                   rq_ref, rk_ref, rv_ref, rg_ref, fq_ref, fk_ref, fv_ref,
                   fkb_ref, fvb_ref, ga_ref, gb_ref, lf8_ref, lfp_ref):
    xb = x_ref[...].astype(BF16)

    def mm(lo, hi):
        return jnp.dot(xb, w_ref[:, lo:hi], preferred_element_type=F32)

    cos = cos_ref[...]
    sin = sin_ref[...]

    def rope(v):
        outs = []
        for h in range(RET_HEADS):
            vh = v[:, h * RET_DK:(h + 1) * RET_DK]
            outs.append(vh * cos + pltpu.roll(vh, RET_DK // 2, 1) * sin)
        return jnp.concatenate(outs, axis=1)

    rq_ref[...] = rope(mm(C_RQ, C_RK)).astype(rq_ref.dtype)
    rk_ref[...] = rope(mm(C_RK, C_RV)) * (RET_DK ** -0.5)
    rv_ref[...] = mm(C_RV, C_RG).astype(rv_ref.dtype)
    rg_ref[...] = mm(C_RG, C_FQ)
    fq_ref[...] = mm(C_FQ, C_FK).astype(fq_ref.dtype)
    fk = mm(C_FK, C_FV)
    fk_ref[...] = fk
    fkb_ref[...] = fk.astype(BF16)
    fv = mm(C_FV, C_GA)
    fv_ref[...] = fv
    fvb_ref[...] = fv.astype(BF16)
    ga_ref[...] = mm(C_GA, C_GB)
    gb_ref[...] = mm(C_GB, C_FF)
    lf = _log_sigmoid(mm(C_FF, C_END) + bf_ref[...])
    lfp_ref[...] = lf
    lf8_ref[...] = lf[:, :FOX_HEADS]


def _inproj(x, w_pack, cos_t, sin_t, bf_pad, *, tm, act_dtype):
    m = x.shape[0]
    n_tab = cos_t.shape[0] // tm
    row = lambda w: pl.BlockSpec((tm, w), lambda i: (i, 0))
    tab = pl.BlockSpec((tm, LANES), lambda i: (i % n_tab, 0))
    out_shape = (
        jax.ShapeDtypeStruct((m, RET_QK_W), act_dtype),
        jax.ShapeDtypeStruct((m, RET_QK_W), F32),
        jax.ShapeDtypeStruct((m, RET_V_W), act_dtype),
        jax.ShapeDtypeStruct((m, RET_V_W), F32),
        jax.ShapeDtypeStruct((m, FOX_W), act_dtype),
        jax.ShapeDtypeStruct((m, FOX_W), F32),
        jax.ShapeDtypeStruct((m, FOX_W), F32),
        jax.ShapeDtypeStruct((m, FOX_W), BF16),
        jax.ShapeDtypeStruct((m, FOX_W), BF16),
        jax.ShapeDtypeStruct((m, D_MODEL), F32),
        jax.ShapeDtypeStruct((m, D_MODEL), F32),
        jax.ShapeDtypeStruct((m, FOX_HEADS), F32),
        jax.ShapeDtypeStruct((m, LANES), F32),
    )
    out_specs = (row(RET_QK_W), row(RET_QK_W), row(RET_V_W), row(RET_V_W), row(FOX_W),
                 row(FOX_W), row(FOX_W), row(FOX_W), row(FOX_W), row(D_MODEL), row(D_MODEL),
                 row(FOX_HEADS), row(LANES))
    return pl.pallas_call(
        _inproj_kernel,
        grid=(m // tm,),
        in_specs=[row(D_MODEL), _resident(w_pack.shape), tab, tab, _resident(bf_pad.shape)],
        out_specs=out_specs,
        out_shape=out_shape,
        compiler_params=pltpu.CompilerParams(
            dimension_semantics=("parallel",), vmem_limit_bytes=56 * 1024 * 1024),
        name="inproj",
    )(x, w_pack, cos_t, sin_t, bf_pad)


def _cumsum_kernel(lf_ref, fcol_ref, frow_ref, *, blk):
    t = lf_ref.shape[0]
    ri = lax.broadcasted_iota(jnp.int32, (blk, blk), 0)
    ci = lax.broadcasted_iota(jnp.int32, (blk, blk), 1)
    tri = (ri >= ci).astype(F32)
    carry = jnp.zeros((1, LANES), F32)
    for b in range(t // blk):
        xb = lf_ref[b * blk:(b + 1) * blk, :]
        fb = jnp.dot(tri, xb, precision=HIGHEST, preferred_element_type=F32) + carry
        fcol_ref[b * blk:(b + 1) * blk, :] = fb
        frow_ref[:, b * blk:(b + 1) * blk] = fb.T[:FOX_HEADS, :]
        carry = fb[blk - 1:blk, :]


def _cumsum(lf_pad, *, batch, seq):
    return pl.pallas_call(
        functools.partial(_cumsum_kernel, blk=LANES),
        grid=(batch,),
        in_specs=[pl.BlockSpec((seq, LANES), lambda b: (b, 0))],
        out_specs=(pl.BlockSpec((seq, LANES), lambda b: (b, 0)),
                   pl.BlockSpec((FOX_HEADS, seq), lambda b: (b, 0))),
        out_shape=(jax.ShapeDtypeStruct((batch * seq, LANES), F32),
                   jax.ShapeDtypeStruct((batch * FOX_HEADS, seq), F32)),
        compiler_params=pltpu.CompilerParams(dimension_semantics=("parallel",)),
        name="logf_cumsum",
    )(lf_pad)


def _ret_head_out(o, gain, rg):
    mu = jnp.mean(o, axis=-1, keepdims=True)
    oc = o - mu
    var = jnp.mean(oc * oc, axis=-1, keepdims=True)
    y = oc * lax.rsqrt(var + GN_EPS) * gain
    return y * (rg * jax.nn.sigmoid(rg))


def _ret_prompt_kernel(q_ref, k_ref, v_ref, rg_ref, gain_ref, o_ref, sfin_ref, s_scr, *, chunk):
    c = pl.program_id(1)

    @pl.when(c == 0)
    def _():
        s_scr[...] = jnp.zeros_like(s_scr)

    ii = lax.broadcasted_iota(jnp.int32, (chunk, chunk), 0)
    jj = lax.broadcasted_iota(jnp.int32, (chunk, chunk), 1)
    causal = ii >= jj
    expo = jnp.where(causal, (ii - jj).astype(F32), 0.0)
    ic = lax.broadcasted_iota(jnp.int32, (chunk, 1), 0).astype(F32)
    for h in range(RET_HEADS):
        ld = _ret_log_decay(h)
        dmat = jnp.where(causal, jnp.exp(expo * ld), 0.0)
        cross_decay = jnp.exp((ic + 1.0) * ld)
        k_decay = jnp.exp((chunk - 1.0 - ic) * ld)
        chunk_decay = math.exp(chunk * ld)
        qh = q_ref[:, h * RET_DK:(h + 1) * RET_DK].astype(BF16)
        kh = k_ref[:, h * RET_DK:(h + 1) * RET_DK]
        vh = v_ref[:, h * RET_DV:(h + 1) * RET_DV].astype(BF16)
        s_old = s_scr[h]
        sc = lax.dot_general(qh, kh.astype(BF16), NT_DIMS, preferred_element_type=F32) * dmat
        intra = jnp.dot(sc.astype(BF16), vh, preferred_element_type=F32)
        cross = jnp.dot(qh, s_old.astype(BF16), preferred_element_type=F32) * cross_decay
        kd = (kh * k_decay).astype(BF16)
        s_scr[h] = s_old * chunk_decay + lax.dot_general(kd, vh, TN_DIMS, preferred_element_type=F32)
        sl = slice(h * RET_DV, (h + 1) * RET_DV)
        o_ref[:, sl] = _ret_head_out(intra + cross, gain_ref[:, sl], rg_ref[:, sl]).astype(o_ref.dtype)

    @pl.when(c == pl.num_programs(1) - 1)
    def _():
        sfin_ref[0] = s_scr[...]


def _ret_prompt(rq, rk, rv, rg, gain, *, batch, seq):
    nc = seq // CHUNK
    row = lambda w: pl.BlockSpec((CHUNK, w), lambda b, c: (b * nc + c, 0))
    return pl.pallas_call(
        functools.partial(_ret_prompt_kernel, chunk=CHUNK),
        grid=(batch, nc),
        in_specs=[row(RET_QK_W), row(RET_QK_W), row(RET_V_W), row(RET_V_W),
                  pl.BlockSpec((1, RET_V_W), lambda b, c: (0, 0))],
        out_specs=(row(RET_V_W),
                   pl.BlockSpec((1, RET_HEADS, RET_DK, RET_DV), lambda b, c: (b, 0, 0, 0))),
        out_shape=(jax.ShapeDtypeStruct((batch * seq, RET_V_W), BF16),
                   jax.ShapeDtypeStruct((batch, RET_HEADS, RET_DK, RET_DV), F32)),
        scratch_shapes=[pltpu.VMEM((RET_HEADS, RET_DK, RET_DV), F32)],
        compiler_params=pltpu.CompilerParams(dimension_semantics=("parallel", "arbitrary")),
        name="ret_prompt",
    )(rq, rk, rv, rg, gain)


def _ret_sample_kernel(q_ref, k_ref, v_ref, rg_ref, gain_ref, s0_ref, o_ref, s1_ref, *, group, t_new):
    rows = group * t_new
    shift = t_new.bit_length() - 1
    ri = lax.broadcasted_iota(jnp.int32, (rows, rows), 0)
    ci = lax.broadcasted_iota(jnp.int32, (rows, rows), 1)
    valid = ((ri >> shift) == (ci >> shift)) & (ri >= ci)
    expo = jnp.where(valid, (ri - ci).astype(F32), 0.0)
    r1 = lax.broadcasted_iota(jnp.int32, (rows, 1), 0)
    tpos = (r1 & (t_new - 1)).astype(F32)
    rowgrp = lax.broadcasted_iota(jnp.int32, (rows, RET_DK), 0) >> shift
    for h in range(RET_HEADS):
        ld = _ret_log_decay(h)
        dmat = jnp.where(valid, jnp.exp(expo * ld), 0.0)
        cross_decay = jnp.exp((tpos + 1.0) * ld)
        k_decay = jnp.exp((t_new - 1.0 - tpos) * ld)
        chunk_decay = math.exp(t_new * ld)
        qh = q_ref[:, h * RET_DK:(h + 1) * RET_DK]
        kh = k_ref[:, h * RET_DK:(h + 1) * RET_DK]
        vh = v_ref[:, h * RET_DV:(h + 1) * RET_DV].astype(BF16)
        qb = qh.astype(BF16)
        sc = lax.dot_general(qb, kh.astype(BF16), NT_DIMS, preferred_element_type=F32) * dmat
        o = jnp.dot(sc.astype(BF16), vh, preferred_element_type=F32)
        kd = kh * k_decay
        cross = jnp.zeros((rows, RET_DV), F32)
        for g in range(group):
            s_old = s0_ref[g, h]
            in_g = rowgrp == g
            qg = jnp.where(in_g, qh, 0.0).astype(BF16)
            cross = cross + jnp.dot(qg, s_old.astype(BF16), preferred_element_type=F32)
            kg = jnp.where(in_g, kd, 0.0).astype(BF16)
            s1_ref[g, h] = s_old * chunk_decay + lax.dot_general(
                kg, vh, TN_DIMS, preferred_element_type=F32)
        o = o + cross * cross_decay
        sl = slice(h * RET_DV, (h + 1) * RET_DV)
        o_ref[:, sl] = _ret_head_out(o, gain_ref[:, sl], rg_ref[:, sl]).astype(o_ref.dtype)


def _ret_sample(rq, rk, rv, rg, gain, state, *, t_new, group):
    m = rq.shape[0]
    rows = group * t_new
    row = lambda w: pl.BlockSpec((rows, w), lambda i: (i, 0))
    st = pl.BlockSpec((group, RET_HEADS, RET_DK, RET_DV), lambda i: (i, 0, 0, 0))
    return pl.pallas_call(
        functools.partial(_ret_sample_kernel, group=group, t_new=t_new),
        grid=(m // rows,),
        in_specs=[row(RET_QK_W), row(RET_QK_W), row(RET_V_W), row(RET_V_W),
                  pl.BlockSpec((1, RET_V_W), lambda i: (0, 0)), st],
        out_specs=(row(RET_V_W), st),
        out_shape=(jax.ShapeDtypeStruct((m, RET_V_W), BF16),
                   jax.ShapeDtypeStruct(state.shape, F32)),
        compiler_params=pltpu.CompilerParams(
            dimension_semantics=("parallel",), vmem_limit_bytes=48 * 1024 * 1024),
        name="ret_sample",
    )(rq, rk, rv, rg, gain, state)


def _fox_prompt_kernel(q_ref, k_ref, v_ref, fcol_ref, frow_ref, o_ref, *, tile, scale):
    h = pl.program_id(1)
    i = pl.program_id(2)
    q = q_ref[...]
    lane = lax.broadcasted_iota(jnp.int32, (tile, LANES), 1)
    fq = jnp.sum(jnp.where(lane == h, fcol_ref[...], 0.0), axis=1, keepdims=True)

    def step(j, carry, masked):
        m_old, l_old, acc = carry
        start = pl.multiple_of(j * tile, tile)
        ks = k_ref[pl.ds(start, tile), :]
        vs = v_ref[pl.ds(start, tile), :]
        fk = frow_ref[0, pl.ds(j, 1), :]
        s = lax.dot_general(q, ks, NT_DIMS, preferred_element_type=F32) * scale + (fq - fk)
        if masked:
            ri = lax.broadcasted_iota(jnp.int32, (tile, tile), 0)
            ci = lax.broadcasted_iota(jnp.int32, (tile, tile), 1)
            s = jnp.where(ci <= ri, s, -jnp.inf)
        m_new = jnp.maximum(m_old, jnp.max(s, axis=1, keepdims=True))
        alpha = jnp.exp(m_old - m_new)
        p = jnp.exp(s - m_new)
        l_new = alpha * l_old + jnp.sum(p, axis=1, keepdims=True)
        acc = alpha * acc + jnp.dot(p.astype(BF16), vs, preferred_element_type=F32)
        return m_new, l_new, acc

    init = (jnp.full((tile, 1), -jnp.inf, F32), jnp.zeros((tile, 1), F32),
            jnp.zeros((tile, FOX_DH), F32))
    carry = lax.fori_loop(0, i, lambda j, c: step(j, c, False), init)
    _, l_fin, acc = step(i, carry, True)
    o_ref[...] = (acc / l_fin).astype(o_ref.dtype)


def _fox_prompt(fq, fkb, fvb, fcol, frow, *, batch, seq, tile):
    nq = seq // tile
    frow3 = frow.reshape(batch * FOX_HEADS, nq, tile)
    return pl.pallas_call(
        functools.partial(_fox_prompt_kernel, tile=tile, scale=FOX_DH ** -0.5),
        grid=(batch, FOX_HEADS, nq),
        in_specs=[pl.BlockSpec((tile, FOX_DH), lambda b, h, i: (b * nq + i, h)),
                  pl.BlockSpec((seq, FOX_DH), lambda b, h, i: (b, h)),
                  pl.BlockSpec((seq, FOX_DH), lambda b, h, i: (b, h)),
                  pl.BlockSpec((tile, LANES), lambda b, h, i: (b * nq + i, 0)),
                  pl.BlockSpec((1, nq, tile), lambda b, h, i: (b * FOX_HEADS + h, 0, 0))],
        out_specs=pl.BlockSpec((tile, FOX_DH), lambda b, h, i: (b * nq + i, h)),
        out_shape=jax.ShapeDtypeStruct((batch * seq, FOX_W), BF16),
        compiler_params=pltpu.CompilerParams(
            dimension_semantics=("parallel", "parallel", "parallel")),
        name="fox_prompt",
    )(fq, fkb, fvb, fcol, frow3)


def _fox_sample_kernel(pt_ref, q_ref, kn_ref, vn_ref, ln_ref, kc_ref, vc_ref, lc_ref, o_ref,
                       m_scr, l_scr, acc_scr, carry_scr, fq_scr, fnrow_scr,
                       *, scale, n_rows, page_cols):
    del pt_ref
    p = pl.program_id(1)
    hmask = FOX_HEADS - 1
    li = lax.broadcasted_iota(jnp.int32, (LANES, LANES), 0)
    lj = lax.broadcasted_iota(jnp.int32, (LANES, LANES), 1)
    same_head = ((li - lj) & hmask) == 0

    @pl.when(p == 0)
    def _():
        m_scr[...] = jnp.full_like(m_scr, -jnp.inf)
        l_scr[...] = jnp.zeros_like(l_scr)
        acc_scr[...] = jnp.zeros_like(acc_scr)
        carry_scr[...] = jnp.zeros_like(carry_scr)
        prefix = (same_head & (li <= lj)).astype(F32)
        xn = jnp.broadcast_to(ln_ref[0], (SUBLANES, LANES))
        fn_row = jnp.dot(xn, prefix, precision=HIGHEST, preferred_element_type=F32)[0:1, :]
        fnrow_scr[...] = fn_row
        eye = (lax.broadcasted_iota(jnp.int32, (n_rows, LANES), 0)
               == lax.broadcasted_iota(jnp.int32, (n_rows, LANES), 1))
        fq_scr[...] = jnp.sum(jnp.where(eye, fn_row, 0.0), axis=1, keepdims=True)

    def online_update(s, v_b):
        m_old = m_scr[...]
        m_new = jnp.maximum(m_old, jnp.max(s, axis=1, keepdims=True))
        alpha = jnp.exp(m_old - m_new)
        pr = jnp.exp(s - m_new)
        l_scr[...] = alpha * l_scr[...] + jnp.sum(pr, axis=1, keepdims=True)
        acc_scr[...] = alpha * acc_scr[...] + jnp.dot(pr.astype(BF16), v_b, preferred_element_type=F32)
        m_scr[...] = m_new

    x = lc_ref[0]
    suffix = (same_head & (li > lj)).astype(F32)
    within = jnp.dot(x, suffix, precision=HIGHEST, preferred_element_type=F32)
    tot = jnp.dot(x, same_head.astype(F32), precision=HIGHEST, preferred_element_type=F32)
    run = carry_scr[...]
    pieces = [None] * SUBLANES
    for r in reversed(range(SUBLANES)):
        pieces[r] = run + within[r:r + 1, :]
        run = run + tot[r:r + 1, :]
    carry_scr[...] = run
    g_row = -jnp.concatenate(pieces, axis=1)

    qb = q_ref[...].astype(BF16)
    fq = fq_scr[...]
    ri = lax.broadcasted_iota(jnp.int32, (n_rows, page_cols), 0)
    ci = lax.broadcasted_iota(jnp.int32, (n_rows, page_cols), 1)
    s = lax.dot_general(qb, kc_ref[0].astype(BF16), NT_DIMS, preferred_element_type=F32)
    s = s * scale + (fq - g_row)
    s = jnp.where(((ri - ci) & hmask) == 0, s, -jnp.inf)
    online_update(s, vc_ref[0].astype(BF16))

    @pl.when(p == pl.num_programs(1) - 1)
    def _():
        zpad = jnp.zeros((LANES - n_rows, FOX_DH), BF16)
        kn = jnp.concatenate([kn_ref[...].astype(BF16), zpad], axis=0)
        vn = jnp.concatenate([vn_ref[...].astype(BF16), zpad], axis=0)
        r2 = lax.broadcasted_iota(jnp.int32, (n_rows, LANES), 0)
        c2 = lax.broadcasted_iota(jnp.int32, (n_rows, LANES), 1)
        ok = (((r2 - c2) & hmask) == 0) & (c2 <= r2)
        s2 = lax.dot_general(qb, kn, NT_DIMS, preferred_element_type=F32)
        s2 = s2 * scale + (fq - fnrow_scr[...])
        s2 = jnp.where(ok, s2, -jnp.inf)
        online_update(s2, vn)
        o_ref[...] = acc_scr[...] / l_scr[...]


def _fox_sample(page_table, fq, fk, fv, lf_new, cache_k, cache_v, cache_lf, *, t_new):
    dec_batch, n_pages = page_table.shape
    n_phys, page_size = cache_k.shape[0], cache_k.shape[1]
    n_rows = t_new * FOX_HEADS
    page_cols = page_size * FOX_HEADS
    q2 = fq.reshape(dec_batch * n_rows, FOX_DH)
    k2 = fk.reshape(dec_batch * n_rows, FOX_DH)
    v2 = fv.reshape(dec_batch * n_rows, FOX_DH)
    ln = jnp.pad(lf_new.reshape(dec_batch, 1, n_rows), ((0, 0), (0, 0), (0, LANES - n_rows)))
    kc = cache_k.reshape(n_phys, page_cols, FOX_DH)
    vc = cache_v.reshape(n_phys, page_cols, FOX_DH)
    lc = cache_lf.reshape(n_phys, page_cols // LANES, LANES)
    new_spec = pl.BlockSpec((n_rows, FOX_DH), lambda b, p, pt: (b, 0))
    page = lambda b, p, pt: (pt[b, n_pages - 1 - p], 0, 0)
    grid_spec = pltpu.PrefetchScalarGridSpec(
        num_scalar_prefetch=1,
        grid=(dec_batch, n_pages),
        in_specs=[new_spec, new_spec, new_spec,
                  pl.BlockSpec((1, 1, LANES), lambda b, p, pt: (b, 0, 0)),
                  pl.BlockSpec((1, page_cols, FOX_DH), page),
                  pl.BlockSpec((1, page_cols, FOX_DH), page),
                  pl.BlockSpec((1, page_cols // LANES, LANES), page)],
        out_specs=new_spec,
        scratch_shapes=[pltpu.VMEM((n_rows, 1), F32), pltpu.VMEM((n_rows, 1), F32),
                        pltpu.VMEM((n_rows, FOX_DH), F32), pltpu.VMEM((1, LANES), F32),
                        pltpu.VMEM((n_rows, 1), F32), pltpu.VMEM((1, LANES), F32)],
    )
    out = pl.pallas_call(
        functools.partial(_fox_sample_kernel, scale=FOX_DH ** -0.5, n_rows=n_rows,
                          page_cols=page_cols),
        grid_spec=grid_spec,
        out_shape=jax.ShapeDtypeStruct((dec_batch * n_rows, FOX_DH), F32),
        compiler_params=pltpu.CompilerParams(dimension_semantics=("parallel", "arbitrary")),
        name="fox_sample",
    )(page_table, q2, k2, v2, ln, kc, vc, lc)
    return out.reshape(dec_batch * t_new, FOX_W)


def _merge_kernel(ro_ref, fo_ref, ga_ref, gb_ref, x_ref, wr_ref, wf_ref, wo_ref, g_ref, b_ref, h_ref):
    a = jnp.dot(ro_ref[...].astype(BF16), wr_ref[...], preferred_element_type=F32)
    b = jnp.dot(fo_ref[...].astype(BF16), wf_ref[...], preferred_element_type=F32)
    merged = jax.nn.sigmoid(ga_ref[...]) * a + jax.nn.sigmoid(gb_ref[...]) * b
    mix = jnp.dot(merged.astype(BF16), wo_ref[...], preferred_element_type=F32)
    h_ref[...] = _layer_norm(ALPHA * x_ref[...] + mix, g_ref[...], b_ref[...])


def _merge(ret_o, fox_o, ga, gb, x, wr, wf, wo, g, b, *, tm):
    m = x.shape[0]
    row = lambda w: pl.BlockSpec((tm, w), lambda i: (i, 0))
    return pl.pallas_call(
        _merge_kernel,
        grid=(m // tm,),
        in_specs=[row(RET_V_W), row(FOX_W), row(D_MODEL), row(D_MODEL), row(D_MODEL),
                  _resident(wr.shape), _resident(wf.shape), _resident(wo.shape),
                  _resident(g.shape), _resident(b.shape)],
        out_specs=row(D_MODEL),
        out_shape=jax.ShapeDtypeStruct((m, D_MODEL), F32),
        compiler_params=pltpu.CompilerParams(
            dimension_semantics=("parallel",), vmem_limit_bytes=48 * 1024 * 1024),
        name="merge_out_ln",
    )(ret_o, fox_o, ga, gb, x, wr, wf, wo, g, b)


def _ffn_kernel(h_ref, wu_ref, wd_ref, g_ref, b_ref, y_ref, *, ff_tile):
    h = h_ref[...]
    hb = h.astype(BF16)
    acc = jnp.zeros(h.shape, F32)
    for c in range(D_FF // ff_tile):
        sl = slice(c * ff_tile, (c + 1) * ff_tile)
        u = jnp.maximum(jnp.dot(hb, wu_ref[:, sl], preferred_element_type=F32), 0.0)
        acc = acc + jnp.dot((u * u).astype(BF16), wd_ref[sl, :], preferred_element_type=F32)
    y_ref[...] = _layer_norm(ALPHA * h + acc, g_ref[...], b_ref[...])


def _ffn(h, wu, wd, g, b, *, tm, ff_tile):
    m = h.shape[0]
    row = pl.BlockSpec((tm, D_MODEL), lambda i: (i, 0))
    return pl.pallas_call(
        functools.partial(_ffn_kernel, ff_tile=ff_tile),
        grid=(m // tm,),
        in_specs=[row, _resident(wu.shape), _resident(wd.shape), _resident(g.shape),
                  _resident(b.shape)],
        out_specs=row,
        out_shape=jax.ShapeDtypeStruct((m, D_MODEL), F32),
        compiler_params=pltpu.CompilerParams(
            dimension_semantics=("parallel",), vmem_limit_bytes=48 * 1024 * 1024),
        name="ffn_ln",
    )(h, wu, wd, g, b)


def _rope_tables(pos):
    half = RET_DK // 2
    inv_freq = ROPE_BASE ** (-jnp.arange(half, dtype=F32) / half)
    ang = pos.astype(F32)[:, None] * inv_freq[None, :]
    c, s = jnp.cos(ang), jnp.sin(ang)
    return jnp.concatenate([c, c], axis=1), jnp.concatenate([-s, s], axis=1)


def kernel(x_prompt, x_sample, cache_k, cache_v, cache_logf, state_ret, page_table, w_in, b_forget,
           ret_gn_gain, w_ret_proj, w_fox_proj, w_out, ln1_g, ln1_b, w_ff_up, w_ff_down, ln2_g,
           ln2_b):
    batch, seq, _ = x_prompt.shape
    dec_batch, t_new, _ = x_sample.shape
    page_size = cache_k.shape[2]
    past_len = page_table.shape[1] * page_size
    tm = 256

    w = w_in[0]
    ff_lo, ff_hi = 2 * RET_QK_W + 2 * RET_V_W + 3 * FOX_W, 2 * RET_QK_W + 2 * RET_V_W + 3 * FOX_W + FOX_HEADS
    w_pack = jnp.concatenate(
        [w[:, :ff_lo], w[:, ff_hi:], jnp.pad(w[:, ff_lo:ff_hi], ((0, 0), (0, LANES - FOX_HEADS)))],
        axis=1).astype(BF16)
    bf_pad = jnp.pad(b_forget[0].astype(F32), (0, LANES - FOX_HEADS)).reshape(1, LANES)
    gain = ret_gn_gain[0].reshape(1, RET_V_W)
    wr, wf, wo = w_ret_proj[0].astype(BF16), w_fox_proj[0].astype(BF16), w_out[0].astype(BF16)
    wu, wd = w_ff_up[0].astype(BF16), w_ff_down[0].astype(BF16)
    g1, b1 = ln1_g[0].reshape(1, D_MODEL), ln1_b[0].reshape(1, D_MODEL)
    g2, b2 = ln2_g[0].reshape(1, D_MODEL), ln2_b[0].reshape(1, D_MODEL)

    def tail(ret_o, fox_o, ga, gb, x2):
        h = _merge(ret_o, fox_o, ga, gb, x2, wr, wf, wo, g1, b1, tm=tm)
        return _ffn(h, wu, wd, g2, b2, tm=tm, ff_tile=512)

    xp = x_prompt.reshape(batch * seq, D_MODEL)
    cos_p, sin_p = _rope_tables(jnp.arange(seq, dtype=jnp.int32))
    (rq, rk, rv, rg, fq, fk, fv, fkb, fvb, ga, gb, lf8, lfp) = _inproj(
        xp, w_pack, cos_p, sin_p, bf_pad, tm=tm, act_dtype=BF16)
    fcol, frow = _cumsum(lfp, batch=batch, seq=seq)
    ret_o, s_prompt = _ret_prompt(rq, rk, rv, rg, gain, batch=batch, seq=seq)
    fox_o = _fox_prompt(fq, fkb, fvb, fcol, frow, batch=batch, seq=seq, tile=512)
    y_prompt = tail(ret_o, fox_o, ga, gb, xp).reshape(batch, seq, D_MODEL)

    xs = x_sample.reshape(dec_batch * t_new, D_MODEL)
    pos_s = past_len + (jnp.arange(dec_batch * t_new, dtype=jnp.int32) % t_new)
    cos_s, sin_s = _rope_tables(pos_s)
    (rq_s, rk_s, rv_s, rg_s, fq_s, fk_s, fv_s, _, _, ga_s, gb_s, lf8_s, _) = _inproj(
        xs, w_pack, cos_s, sin_s, bf_pad, tm=tm, act_dtype=F32)
    ret_o_s, s_sample = _ret_sample(rq_s, rk_s, rv_s, rg_s, gain, state_ret[0], t_new=t_new, group=16)
    fox_o_s = _fox_sample(page_table, fq_s, fk_s, fv_s, lf8_s, cache_k[0], cache_v[0], cache_logf[0],
                          t_new=t_new)
    y_sample = tail(ret_o_s, fox_o_s, ga_s, gb_s, xs).reshape(dec_batch, t_new, D_MODEL)

    return (y_prompt, y_sample,
            fk.reshape(1, batch, seq, FOX_HEADS, FOX_DH), fv.reshape(1, batch, seq, FOX_HEADS, FOX_DH),
            lf8.reshape(1, batch, seq, FOX_HEADS), s_prompt[None],
            fk_s.reshape(1, dec_batch, t_new, FOX_HEADS, FOX_DH),
            fv_s.reshape(1, dec_batch, t_new, FOX_HEADS, FOX_DH),
            lf8_s.reshape(1, dec_batch, t_new, FOX_HEADS), s_sample[None])
```

```python
import functools
import math

import jax
import jax.numpy as jnp
from jax import lax
from jax.experimental import pallas as pl
from jax.experimental.pallas import tpu as pltpu

F32 = jnp.float32
BF16 = jnp.bfloat16
HIGHEST = lax.Precision.HIGHEST

D_MODEL = 1024
RET_HEADS = 4
RET_DK = 128
RET_DV = 256
FOX_HEADS = 8
FOX_DH = 128
D_FF = 4 * D_MODEL
CHUNK = 128
ROPE_BASE = 10000.0
LN_EPS = 1e-5
GN_EPS = 1e-6
DEPTH = 1
ALPHA = (2.0 * DEPTH) ** 0.25
RET_QK_W = RET_HEADS * RET_DK
RET_V_W = RET_HEADS * RET_DV
FOX_W = FOX_HEADS * FOX_DH
LOG2E = math.log2(math.e)
LANES = 128
SUBLANES = 8

NT_DIMS = (((1,), (1,)), ((), ()))
TN_DIMS = (((0,), (0,)), ((), ()))

C_RQ, C_RK, C_RV, C_RG, C_FQ, C_FK, C_FV, C_GA, C_GB, C_FF, C_END = (
    0, 512, 1024, 2048, 3072, 4096, 5120, 6144, 7168, 8192, 8320)


def _resident(shape):
    return pl.BlockSpec(shape, lambda *_: (0,) * len(shape), pipeline_mode=pl.Buffered(1))


def _log_sigmoid(z):
    return jnp.minimum(z, 0.0) - jnp.log1p(jnp.exp(-jnp.abs(z)))


def _layer_norm(z, g, b):
    mu = jnp.mean(z, axis=-1, keepdims=True)
    zc = z - mu
    var = jnp.mean(zc * zc, axis=-1, keepdims=True)
    return zc * lax.rsqrt(var + LN_EPS) * g + b


def _ret_log_decay(h):
    return math.log(1.0 - 2.0 ** (-5.0 - h))


def _inproj_kernel(x_ref, w_ref, cos_ref, sin_ref, bf_ref,
                   rq_ref, rk_ref, rv_ref, rg_ref, fq_ref, fk_ref, fv_ref,
                   ga_ref, gb_ref, lf8_ref, lfp_ref, *extra_refs):
    xb = x_ref[...].astype(BF16)

    def mm(lo, hi):
        return jnp.dot(xb, w_ref[:, lo:hi], preferred_element_type=F32)

    cos = cos_ref[...]
    sin = sin_ref[...]

    def rope(v):
        outs = []
        for h in range(RET_HEADS):
            vh = v[:, h * RET_DK:(h + 1) * RET_DK]
            outs.append(vh * cos + pltpu.roll(vh, RET_DK // 2, 1) * sin)
        return jnp.concatenate(outs, axis=1)

    rq_ref[...] = rope(mm(C_RQ, C_RK)).astype(rq_ref.dtype)
    rk_ref[...] = rope(mm(C_RK, C_RV)) * (RET_DK ** -0.5)
    rv_ref[...] = mm(C_RV, C_RG).astype(rv_ref.dtype)
    rg_ref[...] = mm(C_RG, C_FQ)
    fq_ref[...] = mm(C_FQ, C_FK).astype(fq_ref.dtype)
    fk = mm(C_FK, C_FV)
    fk_ref[...] = fk
    fv = mm(C_FV, C_GA)
    fv_ref[...] = fv
    if extra_refs:
        fkb_ref, fvt_ref = extra_refs
        fkb_ref[...] = fk.astype(BF16)
        for h in range(FOX_HEADS):
            fvt_ref[0, h, 0] = fv[:, h * FOX_DH:(h + 1) * FOX_DH].T.astype(BF16)
    ga_ref[...] = mm(C_GA, C_GB)
    gb_ref[...] = mm(C_GB, C_FF)
    lf = _log_sigmoid(mm(C_FF, C_END) + bf_ref[...])
    lfp_ref[...] = lf
    lf8_ref[...] = lf[:, :FOX_HEADS]


def _inproj(x, w_pack, cos_t, sin_t, bf_pad, *, tm, act_dtype, prompt_seq=None):
    m = x.shape[0]
    n_tab = cos_t.shape[0] // tm
    row = lambda w: pl.BlockSpec((tm, w), lambda i: (i, 0))
    tab = pl.BlockSpec((tm, LANES), lambda i: (i % n_tab, 0))
    out_shape = [
        jax.ShapeDtypeStruct((m, RET_QK_W), act_dtype),
        jax.ShapeDtypeStruct((m, RET_QK_W), F32),
        jax.ShapeDtypeStruct((m, RET_V_W), act_dtype),
        jax.ShapeDtypeStruct((m, RET_V_W), F32),
        jax.ShapeDtypeStruct((m, FOX_W), act_dtype),
        jax.ShapeDtypeStruct((m, FOX_W), F32),
        jax.ShapeDtypeStruct((m, FOX_W), F32),
        jax.ShapeDtypeStruct((m, D_MODEL), F32),
        jax.ShapeDtypeStruct((m, D_MODEL), F32),
        jax.ShapeDtypeStruct((m, FOX_HEADS), F32),
        jax.ShapeDtypeStruct((m, LANES), F32),
    ]
    out_specs = [row(RET_QK_W), row(RET_QK_W), row(RET_V_W), row(RET_V_W), row(FOX_W),
                 row(FOX_W), row(FOX_W), row(D_MODEL), row(D_MODEL), row(FOX_HEADS), row(LANES)]
    if prompt_seq is not None:
        nt = prompt_seq // tm
        out_shape += [
            jax.ShapeDtypeStruct((m, FOX_W), BF16),
            jax.ShapeDtypeStruct((m // prompt_seq, FOX_HEADS, nt, FOX_DH, tm), BF16),
        ]
        out_specs += [row(FOX_W),
                      pl.BlockSpec((1, FOX_HEADS, 1, FOX_DH, tm), lambda i: (i // nt, 0, i % nt, 0, 0))]
    return pl.pallas_call(
        _inproj_kernel,
        grid=(m // tm,),
        in_specs=[row(D_MODEL), _resident(w_pack.shape), tab, tab, _resident(bf_pad.shape)],
        out_specs=out_specs,
        out_shape=out_shape,
        compiler_params=pltpu.CompilerParams(
            dimension_semantics=("parallel",), vmem_limit_bytes=56 * 1024 * 1024),
        name="inproj",
    )(x, w_pack, cos_t, sin_t, bf_pad)


def _cumsum_kernel(lf_ref, fcol_ref, frow_ref, *, blk):
    t = lf_ref.shape[0]
    ri = lax.broadcasted_iota(jnp.int32, (blk, blk), 0)
    ci = lax.broadcasted_iota(jnp.int32, (blk, blk), 1)
    tri = (ri >= ci).astype(F32)
    carry = jnp.zeros((1, LANES), F32)
    for b in range(t // blk):
        xb = lf_ref[b * blk:(b + 1) * blk, :]
        fb = jnp.dot(tri, xb, precision=HIGHEST, preferred_element_type=F32) + carry
        fcol_ref[b * blk:(b + 1) * blk, :] = fb
        frow_ref[:, b * blk:(b + 1) * blk] = fb.T[:FOX_HEADS, :]
        carry = fb[blk - 1:blk, :]


def _cumsum(lf_pad, *, batch, seq):
    return pl.pallas_call(
        functools.partial(_cumsum_kernel, blk=LANES),
        grid=(batch,),
        in_specs=[pl.BlockSpec((seq, LANES), lambda b: (b, 0))],
        out_specs=(pl.BlockSpec((seq, LANES), lambda b: (b, 0)),
                   pl.BlockSpec((FOX_HEADS, seq), lambda b: (b, 0))),
        out_shape=(jax.ShapeDtypeStruct((batch * seq, LANES), F32),
                   jax.ShapeDtypeStruct((batch * FOX_HEADS, seq), F32)),
        compiler_params=pltpu.CompilerParams(dimension_semantics=("parallel",)),
        name="logf_cumsum",
    )(lf_pad)


def _ret_head_out(o, gain, rg):
    mu = jnp.mean(o, axis=-1, keepdims=True)
    oc = o - mu
    var = jnp.mean(oc * oc, axis=-1, keepdims=True)
    y = oc * lax.rsqrt(var + GN_EPS) * gain
    return y * (rg * jax.nn.sigmoid(rg))


def _ret_prompt_kernel(q_ref, k_ref, v_ref, rg_ref, gain_ref, o_ref, sfin_ref, s_scr, *, chunk):
    c = pl.program_id(1)

    @pl.when(c == 0)
    def _():
        s_scr[...] = jnp.zeros_like(s_scr)

    ii = lax.broadcasted_iota(jnp.int32, (chunk, chunk), 0)
    jj = lax.broadcasted_iota(jnp.int32, (chunk, chunk), 1)
    causal = ii >= jj
    expo = jnp.where(causal, (ii - jj).astype(F32), 0.0)
    ic = lax.broadcasted_iota(jnp.int32, (chunk, 1), 0).astype(F32)
    for h in range(RET_HEADS):
        ld = _ret_log_decay(h)
        dmat = jnp.where(causal, jnp.exp(expo * ld), 0.0)
        cross_decay = jnp.exp((ic + 1.0) * ld)
        k_decay = jnp.exp((chunk - 1.0 - ic) * ld)
        chunk_decay = math.exp(chunk * ld)
        qh = q_ref[:, h * RET_DK:(h + 1) * RET_DK].astype(BF16)
        kh = k_ref[:, h * RET_DK:(h + 1) * RET_DK]
        vh = v_ref[:, h * RET_DV:(h + 1) * RET_DV].astype(BF16)
        s_old = s_scr[h]
        sc = lax.dot_general(qh, kh.astype(BF16), NT_DIMS, preferred_element_type=F32) * dmat
        intra = jnp.dot(sc.astype(BF16), vh, preferred_element_type=F32)
        cross = jnp.dot(qh, s_old.astype(BF16), preferred_element_type=F32) * cross_decay
        kd = (kh * k_decay).astype(BF16)
        s_scr[h] = s_old * chunk_decay + lax.dot_general(kd, vh, TN_DIMS, preferred_element_type=F32)
        sl = slice(h * RET_DV, (h + 1) * RET_DV)
        o_ref[:, sl] = _ret_head_out(intra + cross, gain_ref[:, sl], rg_ref[:, sl]).astype(o_ref.dtype)

    @pl.when(c == pl.num_programs(1) - 1)
    def _():
        sfin_ref[0] = s_scr[...]


def _ret_prompt(rq, rk, rv, rg, gain, *, batch, seq):
    nc = seq // CHUNK
    row = lambda w: pl.BlockSpec((CHUNK, w), lambda b, c: (b * nc + c, 0))
    return pl.pallas_call(
        functools.partial(_ret_prompt_kernel, chunk=CHUNK),
        grid=(batch, nc),
        in_specs=[row(RET_QK_W), row(RET_QK_W), row(RET_V_W), row(RET_V_W),
                  pl.BlockSpec((1, RET_V_W), lambda b, c: (0, 0))],
        out_specs=(row(RET_V_W),
                   pl.BlockSpec((1, RET_HEADS, RET_DK, RET_DV), lambda b, c: (b, 0, 0, 0))),
        out_shape=(jax.ShapeDtypeStruct((batch * seq, RET_V_W), BF16),
                   jax.ShapeDtypeStruct((batch, RET_HEADS, RET_DK, RET_DV), F32)),
        scratch_shapes=[pltpu.VMEM((RET_HEADS, RET_DK, RET_DV), F32)],
        compiler_params=pltpu.CompilerParams(dimension_semantics=("parallel", "arbitrary")),
        name="ret_prompt",
    )(rq, rk, rv, rg, gain)


def _ret_sample_kernel(q_ref, k_ref, v_ref, rg_ref, gain_ref, s0_ref, o_ref, s1_ref, *, group, t_new):
    rows = group * t_new
    shift = t_new.bit_length() - 1
    ri = lax.broadcasted_iota(jnp.int32, (rows, rows), 0)
    ci = lax.broadcasted_iota(jnp.int32, (rows, rows), 1)
    valid = ((ri >> shift) == (ci >> shift)) & (ri >= ci)
    expo = jnp.where(valid, (ri - ci).astype(F32), 0.0)
    r1 = lax.broadcasted_iota(jnp.int32, (rows, 1), 0)
    tpos = (r1 & (t_new - 1)).astype(F32)
    rowgrp = lax.broadcasted_iota(jnp.int32, (rows, RET_DK), 0) >> shift
    for h in range(RET_HEADS):
        ld = _ret_log_decay(h)
        dmat = jnp.where(valid, jnp.exp(expo * ld), 0.0)
        cross_decay = jnp.exp((tpos + 1.0) * ld)
        k_decay = jnp.exp((t_new - 1.0 - tpos) * ld)
        chunk_decay = math.exp(t_new * ld)
        qh = q_ref[:, h * RET_DK:(h + 1) * RET_DK]
        kh = k_ref[:, h * RET_DK:(h + 1) * RET_DK]
        vh = v_ref[:, h * RET_DV:(h + 1) * RET_DV].astype(BF16)
        qb = qh.astype(BF16)
        sc = lax.dot_general(qb, kh.astype(BF16), NT_DIMS, preferred_element_type=F32) * dmat
        o = jnp.dot(sc.astype(BF16), vh, preferred_element_type=F32)
        kd = kh * k_decay
        cross = jnp.zeros((rows, RET_DV), F32)
        for g in range(group):
            s_old = s0_ref[g, h]
            in_g = rowgrp == g
            qg = jnp.where(in_g, qh, 0.0).astype(BF16)
            cross = cross + jnp.dot(qg, s_old.astype(BF16), preferred_element_type=F32)
            kg = jnp.where(in_g, kd, 0.0).astype(BF16)
            s1_ref[g, h] = s_old * chunk_decay + lax.dot_general(
                kg, vh, TN_DIMS, preferred_element_type=F32)
        o = o + cross * cross_decay
        sl = slice(h * RET_DV, (h + 1) * RET_DV)
        o_ref[:, sl] = _ret_head_out(o, gain_ref[:, sl], rg_ref[:, sl]).astype(o_ref.dtype)


def _ret_sample(rq, rk, rv, rg, gain, state, *, t_new, group):
    m = rq.shape[0]
    rows = group * t_new
    row = lambda w: pl.BlockSpec((rows, w), lambda i: (i, 0))
    st = pl.BlockSpec((group, RET_HEADS, RET_DK, RET_DV), lambda i: (i, 0, 0, 0))
    return pl.pallas_call(
        functools.partial(_ret_sample_kernel, group=group, t_new=t_new),
        grid=(m // rows,),
        in_specs=[row(RET_QK_W), row(RET_QK_W), row(RET_V_W), row(RET_V_W),
                  pl.BlockSpec((1, RET_V_W), lambda i: (0, 0)), st],
        out_specs=(row(RET_V_W), st),
        out_shape=(jax.ShapeDtypeStruct((m, RET_V_W), BF16),
                   jax.ShapeDtypeStruct(state.shape, F32)),
        compiler_params=pltpu.CompilerParams(
            dimension_semantics=("parallel",), vmem_limit_bytes=48 * 1024 * 1024),
        name="ret_sample",
    )(rq, rk, rv, rg, gain, state)


def _fox_prompt_kernel(q_ref, k_ref, vt_ref, fcum_ref, frow_ref, o_ref,
                       fk_scr, t0_scr, t1_scr, p0_scr, p1_scr, *, tile, scale):
    h = pl.program_id(1)
    i = pl.program_id(2)
    seq = k_ref.shape[0]
    vt_tile = vt_ref.shape[-1]
    per = tile // vt_tile
    assert per % 2 == 0

    @pl.when(i == 0)
    def _():
        lane = lax.broadcasted_iota(jnp.int32, (seq, LANES), 1)
        col = jnp.sum(jnp.where(lane == h, fcum_ref[...], 0.0), axis=1, keepdims=True)
        fk_scr[...] = jnp.broadcast_to(col * LOG2E, (seq, LANES))

    q = q_ref[...]
    fq2 = frow_ref[0, pl.ds(i, 1), :] * LOG2E
    t_scr = (t0_scr, t1_scr)
    p_scr = (p0_scr, p1_scr)

    def scores(u):
        start = pl.multiple_of(u * vt_tile, vt_tile)
        ks = k_ref[pl.ds(start, vt_tile), :]
        fk2 = fk_scr[pl.ds(start, vt_tile), :]
        t = lax.dot_general(ks, q, NT_DIMS, preferred_element_type=F32) * (scale * LOG2E)
        return t - jnp.concatenate([fk2] * (tile // LANES), axis=1)

    def sub_step(u, slot, state, *, diag_part=None, last=False):
        m_run, l_run, acc, alpha_prev = state
        t = t_scr[slot][...]
        if not last:
            t_scr[1 - slot][...] = scores(u + 1)
        pv_prev = jnp.dot(vt_ref[0, 0, jnp.maximum(u - 1, 0)], p_scr[1 - slot][...],
                          preferred_element_type=F32)
        acc = alpha_prev * acc + pv_prev
        if diag_part is not None:
            key = diag_part * vt_tile + lax.broadcasted_iota(jnp.int32, (vt_tile, tile), 0)
            qry = lax.broadcasted_iota(jnp.int32, (vt_tile, tile), 1)
            t = jnp.where(key <= qry, t, -jnp.inf)
        m_new = jnp.maximum(m_run, jnp.max(t, axis=0, keepdims=True) + fq2)
        alpha = jnp.exp2(m_run - m_new)
        p = jnp.exp2(t + (fq2 - m_new))
        l_new = alpha * l_run + jnp.sum(p, axis=0, keepdims=True)
        p_scr[slot][...] = p.astype(BF16)
        return m_new, l_new, acc, alpha

    def pair(j, state):
        for s in range(per):
            state = sub_step(j * per + s, s % 2, state)
        return state

    t_scr[0][...] = scores(0)
    p_scr[1][...] = jnp.zeros_like(p_scr[1])
    state = (jnp.full((1, tile), -jnp.inf, F32), jnp.zeros((1, tile), F32),
             jnp.zeros((FOX_DH, tile), F32), jnp.ones((1, tile), F32))
    state = lax.fori_loop(0, i, pair, state)
    for s in range(per):
        state = sub_step(i * per + s, s % 2, state, diag_part=s, last=(s == per - 1))
    _, l_fin, acc, alpha_prev = state
    last_u = i * per + per - 1
    acc = alpha_prev * acc + jnp.dot(vt_ref[0, 0, last_u], p_scr[(per - 1) % 2][...],
                                     preferred_element_type=F32)
    o_ref[...] = (acc / l_fin).T.astype(o_ref.dtype)


def _fox_prompt(fq, fkb, fvt, fcum, frow, *, batch, seq, tile):
    nq = seq // tile
    frow3 = frow.reshape(batch * FOX_HEADS, nq, tile)
    nt, vt_tile = fvt.shape[2], fvt.shape[4]
    return pl.pallas_call(
        functools.partial(_fox_prompt_kernel, tile=tile, scale=FOX_DH ** -0.5),
        grid=(batch, FOX_HEADS, nq),
        in_specs=[pl.BlockSpec((tile, FOX_DH), lambda b, h, i: (b * nq + i, h)),
                  pl.BlockSpec((seq, FOX_DH), lambda b, h, i: (b, h)),
                  pl.BlockSpec((1, 1, nt, FOX_DH, vt_tile), lambda b, h, i: (b, h, 0, 0, 0)),
                  pl.BlockSpec((seq, LANES), lambda b, h, i: (b, 0)),
                  pl.BlockSpec((1, nq, tile), lambda b, h, i: (b * FOX_HEADS + h, 0, 0))],
        out_specs=pl.BlockSpec((tile, FOX_DH), lambda b, h, i: (b * nq + i, h)),
        out_shape=jax.ShapeDtypeStruct((batch * seq, FOX_W), BF16),
        scratch_shapes=[pltpu.VMEM((seq, LANES), F32),
                        pltpu.VMEM((vt_tile, tile), F32), pltpu.VMEM((vt_tile, tile), F32),
                        pltpu.VMEM((vt_tile, tile), BF16), pltpu.VMEM((vt_tile, tile), BF16)],
        compiler_params=pltpu.CompilerParams(
            dimension_semantics=("parallel", "parallel", "arbitrary")),
        name="fox_prompt",
    )(fq, fkb, fvt, fcum, frow3)


def _fox_sample_kernel(pt_ref, q_ref, kn_ref, vn_ref, ln_ref, *refs, n_pages, scale, n_rows, page_cols):
    del pt_ref
    kc_refs = refs[:n_pages]
    vc_refs = refs[n_pages:2 * n_pages]
    lc_refs = refs[2 * n_pages:3 * n_pages]
    o_ref, s_scr = refs[3 * n_pages:]
    hmask = FOX_HEADS - 1
    li = lax.broadcasted_iota(jnp.int32, (LANES, LANES), 0)
    lj = lax.broadcasted_iota(jnp.int32, (LANES, LANES), 1)
    same_head = ((li - lj) & hmask) == 0

    prefix = (same_head & (li <= lj)).astype(F32)
    xn = jnp.broadcast_to(ln_ref[0], (SUBLANES, LANES))
    fn_row = jnp.dot(xn, prefix, precision=HIGHEST, preferred_element_type=F32)[0:1, :]
    eye = (lax.broadcasted_iota(jnp.int32, (n_rows, LANES), 0)
           == lax.broadcasted_iota(jnp.int32, (n_rows, LANES), 1))
    fq = jnp.sum(jnp.where(eye, fn_row, 0.0), axis=1, keepdims=True)

    suffix = (same_head & (li > lj)).astype(F32)
    same_f = same_head.astype(F32)
    run = jnp.zeros((1, LANES), F32)
    g_rows = [None] * n_pages
    for pg in reversed(range(n_pages)):
        x = lc_refs[pg][0]
        within = jnp.dot(x, suffix, precision=HIGHEST, preferred_element_type=F32)
        tot = jnp.dot(x, same_f, precision=HIGHEST, preferred_element_type=F32)
        pieces = [None] * SUBLANES
        for r in reversed(range(SUBLANES)):
            pieces[r] = run + within[r:r + 1, :]
            run = run + tot[r:r + 1, :]
        g_rows[pg] = -jnp.concatenate(pieces, axis=1)

    qb = q_ref[...].astype(BF16)
    ri = lax.broadcasted_iota(jnp.int32, (n_rows, page_cols), 0)
    ci = lax.broadcasted_iota(jnp.int32, (n_rows, page_cols), 1)
    head_ok = ((ri - ci) & hmask) == 0
    m = jnp.full((n_rows, 1), -jnp.inf, F32)
    for pg in range(n_pages):
        s = lax.dot_general(qb, kc_refs[pg][0].astype(BF16), NT_DIMS, preferred_element_type=F32)
        s = jnp.where(head_ok, s * scale + (fq - g_rows[pg]), -jnp.inf)
        s_scr[pg] = s
        m = jnp.maximum(m, jnp.max(s, axis=1, keepdims=True))

    zpad = jnp.zeros((LANES - n_rows, FOX_DH), BF16)
    kn = jnp.concatenate([kn_ref[...].astype(BF16), zpad], axis=0)
    vn = jnp.concatenate([vn_ref[...].astype(BF16), zpad], axis=0)
    r2 = lax.broadcasted_iota(jnp.int32, (n_rows, LANES), 0)
    c2 = lax.broadcasted_iota(jnp.int32, (n_rows, LANES), 1)
    ok = (((r2 - c2) & hmask) == 0) & (c2 <= r2)
    s2 = lax.dot_general(qb, kn, NT_DIMS, preferred_element_type=F32)
    s2 = jnp.where(ok, s2 * scale + (fq - fn_row), -jnp.inf)
    m = jnp.maximum(m, jnp.max(s2, axis=1, keepdims=True))

    l = jnp.zeros((n_rows, 1), F32)
    acc = jnp.zeros((n_rows, FOX_DH), F32)
    for pg in range(n_pages):
        pr = jnp.exp(s_scr[pg] - m)
        l = l + jnp.sum(pr, axis=1, keepdims=True)
        acc = acc + jnp.dot(pr.astype(BF16), vc_refs[pg][0].astype(BF16), preferred_element_type=F32)
    p2 = jnp.exp(s2 - m)
    l = l + jnp.sum(p2, axis=1, keepdims=True)
    acc = acc + jnp.dot(p2.astype(BF16), vn, preferred_element_type=F32)
    o_ref[...] = acc / l


def _page_index(b, pt, *, j):
    return (pt[b, j], 0, 0)


def _fox_sample(page_table, fq, fk, fv, lf_new, cache_k, cache_v, cache_lf, *, t_new):
    dec_batch, n_pages = page_table.shape
    n_phys, page_size = cache_k.shape[0], cache_k.shape[1]
    n_rows = t_new * FOX_HEADS
    page_cols = page_size * FOX_HEADS
    q2 = fq.reshape(dec_batch * n_rows, FOX_DH)
    k2 = fk.reshape(dec_batch * n_rows, FOX_DH)
    v2 = fv.reshape(dec_batch * n_rows, FOX_DH)
    ln = jnp.pad(lf_new.reshape(dec_batch, 1, n_rows), ((0, 0), (0, 0), (0, LANES - n_rows)))
    kc = cache_k.reshape(n_phys, page_cols, FOX_DH)
    vc = cache_v.reshape(n_phys, page_cols, FOX_DH)
    lc = cache_lf.reshape(n_phys, page_cols // LANES, LANES)
    new_spec = pl.BlockSpec((n_rows, FOX_DH), lambda b, pt: (b, 0))
    kv_specs = [pl.BlockSpec((1, page_cols, FOX_DH), functools.partial(_page_index, j=j))
                for j in range(n_pages)]
    lf_specs = [pl.BlockSpec((1, page_cols // LANES, LANES), functools.partial(_page_index, j=j))
                for j in range(n_pages)]
    grid_spec = pltpu.PrefetchScalarGridSpec(
        num_scalar_prefetch=1,
        grid=(dec_batch,),
        in_specs=[new_spec, new_spec, new_spec, pl.BlockSpec((1, 1, LANES), lambda b, pt: (b, 0, 0))]
        + kv_specs + kv_specs + lf_specs,
        out_specs=new_spec,
        scratch_shapes=[pltpu.VMEM((n_pages, n_rows, page_cols), F32)],
    )
    out = pl.pallas_call(
        functools.partial(_fox_sample_kernel, n_pages=n_pages, scale=FOX_DH ** -0.5, n_rows=n_rows,
                          page_cols=page_cols),
        grid_spec=grid_spec,
        out_shape=jax.ShapeDtypeStruct((dec_batch * n_rows, FOX_DH), F32),
        compiler_params=pltpu.CompilerParams(
            dimension_semantics=("parallel",), vmem_limit_bytes=56 * 1024 * 1024),
        name="fox_sample",
    )(page_table, q2, k2, v2, ln, *([kc] * n_pages), *([vc] * n_pages), *([lc] * n_pages))
    return out.reshape(dec_batch * t_new, FOX_W)


def _merge_kernel(ro_ref, fo_ref, ga_ref, gb_ref, x_ref, wr_ref, wf_ref, wo_ref, g_ref, b_ref, h_ref):
    a = jnp.dot(ro_ref[...].astype(BF16), wr_ref[...], preferred_element_type=F32)
    b = jnp.dot(fo_ref[...].astype(BF16), wf_ref[...], preferred_element_type=F32)
    merged = jax.nn.sigmoid(ga_ref[...]) * a + jax.nn.sigmoid(gb_ref[...]) * b
    mix = jnp.dot(merged.astype(BF16), wo_ref[...], preferred_element_type=F32)
    h_ref[...] = _layer_norm(ALPHA * x_ref[...] + mix, g_ref[...], b_ref[...])


def _merge(ret_o, fox_o, ga, gb, x, wr, wf, wo, g, b, *, tm):
    m = x.shape[0]
    row = lambda w: pl.BlockSpec((tm, w), lambda i: (i, 0))
    return pl.pallas_call(
        _merge_kernel,
        grid=(m // tm,),
        in_specs=[row(RET_V_W), row(FOX_W), row(D_MODEL), row(D_MODEL), row(D_MODEL),
                  _resident(wr.shape), _resident(wf.shape), _resident(wo.shape),
                  _resident(g.shape), _resident(b.shape)],
        out_specs=row(D_MODEL),
        out_shape=jax.ShapeDtypeStruct((m, D_MODEL), F32),
        compiler_params=pltpu.CompilerParams(
            dimension_semantics=("parallel",), vmem_limit_bytes=48 * 1024 * 1024),
        name="merge_out_ln",
    )(ret_o, fox_o, ga, gb, x, wr, wf, wo, g, b)


def _ffn_kernel(h_ref, wu_ref, wd_ref, g_ref, b_ref, y_ref, *, ff_tile):
    h = h_ref[...]
    hb = h.astype(BF16)
    acc = jnp.zeros(h.shape, F32)
    for c in range(D_FF // ff_tile):
        sl = slice(c * ff_tile, (c + 1) * ff_tile)
        u = jnp.maximum(jnp.dot(hb, wu_ref[:, sl], preferred_element_type=F32), 0.0)
        acc = acc + jnp.dot((u * u).astype(BF16), wd_ref[sl, :], preferred_element_type=F32)
    y_ref[...] = _layer_norm(ALPHA * h + acc, g_ref[...], b_ref[...])


def _ffn(h, wu, wd, g, b, *, tm, ff_tile):
    m = h.shape[0]
    row = pl.BlockSpec((tm, D_MODEL), lambda i: (i, 0))
    return pl.pallas_call(
        functools.partial(_ffn_kernel, ff_tile=ff_tile),
        grid=(m // tm,),
        in_specs=[row, _resident(wu.shape), _resident(wd.shape), _resident(g.shape),
                  _resident(b.shape)],
        out_specs=row,
        out_shape=jax.ShapeDtypeStruct((m, D_MODEL), F32),
        compiler_params=pltpu.CompilerParams(
            dimension_semantics=("parallel",), vmem_limit_bytes=48 * 1024 * 1024),
        name="ffn_ln",
    )(h, wu, wd, g, b)


def _rope_tables(pos):
    half = RET_DK // 2
    inv_freq = ROPE_BASE ** (-jnp.arange(half, dtype=F32) / half)
    ang = pos.astype(F32)[:, None] * inv_freq[None, :]
    c, s = jnp.cos(ang), jnp.sin(ang)
    return jnp.concatenate([c, c], axis=1), jnp.concatenate([-s, s], axis=1)


def kernel(x_prompt, x_sample, cache_k, cache_v, cache_logf, state_ret, page_table, w_in, b_forget,
           ret_gn_gain, w_ret_proj, w_fox_proj, w_out, ln1_g, ln1_b, w_ff_up, w_ff_down, ln2_g,
           ln2_b):
    batch, seq, _ = x_prompt.shape
    dec_batch, t_new, _ = x_sample.shape
    page_size = cache_k.shape[2]
    past_len = page_table.shape[1] * page_size
    tm = 256

    w = w_in[0]
    ff_lo, ff_hi = 2 * RET_QK_W + 2 * RET_V_W + 3 * FOX_W, 2 * RET_QK_W + 2 * RET_V_W + 3 * FOX_W + FOX_HEADS
    w_pack = jnp.concatenate(
        [w[:, :ff_lo], w[:, ff_hi:], jnp.pad(w[:, ff_lo:ff_hi], ((0, 0), (0, LANES - FOX_HEADS)))],
        axis=1).astype(BF16)
    bf_pad = jnp.pad(b_forget[0].astype(F32), (0, LANES - FOX_HEADS)).reshape(1, LANES)
    gain = ret_gn_gain[0].reshape(1, RET_V_W)
    wr, wf, wo = w_ret_proj[0].astype(BF16), w_fox_proj[0].astype(BF16), w_out[0].astype(BF16)
    wu, wd = w_ff_up[0].astype(BF16), w_ff_down[0].astype(BF16)
    g1, b1 = ln1_g[0].reshape(1, D_MODEL), ln1_b[0].reshape(1, D_MODEL)
    g2, b2 = ln2_g[0].reshape(1, D_MODEL), ln2_b[0].reshape(1, D_MODEL)

    def tail(ret_o, fox_o, ga, gb, x2):
        h = _merge(ret_o, fox_o, ga, gb, x2, wr, wf, wo, g1, b1, tm=tm)
        return _ffn(h, wu, wd, g2, b2, tm=tm, ff_tile=512)

    xp = x_prompt.reshape(batch * seq, D_MODEL)
    cos_p, sin_p = _rope_tables(jnp.arange(seq, dtype=jnp.int32))
    (rq, rk, rv, rg, fq, fk, fv, ga, gb, lf8, lfp, fkb, fvt) = _inproj(
        xp, w_pack, cos_p, sin_p, bf_pad, tm=tm, act_dtype=BF16, prompt_seq=seq)
    fcum, frow = _cumsum(lfp, batch=batch, seq=seq)
    ret_o, s_prompt = _ret_prompt(rq, rk, rv, rg, gain, batch=batch, seq=seq)
    fox_o = _fox_prompt(fq, fkb, fvt, fcum, frow, batch=batch, seq=seq, tile=512)
    y_prompt = tail(ret_o, fox_o, ga, gb, xp).reshape(batch, seq, D_MODEL)

    xs = x_sample.reshape(dec_batch * t_new, D_MODEL)
    pos_s = past_len + (jnp.arange(dec_batch * t_new, dtype=jnp.int32) % t_new)
    cos_s, sin_s = _rope_tables(pos_s)
    (rq_s, rk_s, rv_s, rg_s, fq_s, fk_s, fv_s, ga_s, gb_s, lf8_s, _) = _inproj(
        xs, w_pack, cos_s, sin_s, bf_pad, tm=tm, act_dtype=F32)
    ret_o_s, s_sample = _ret_sample(rq_s, rk_s, rv_s, rg_s, gain, state_ret[0], t_new=t_new, group=16)
    fox_o_s = _fox_sample(page_table, fq_s, fk_s, fv_s, lf8_s, cache_k[0], cache_v[0], cache_logf[0],
                          t_new=t_new)
    y_sample = tail(ret_o_s, fox_o_s, ga_s, gb_s, xs).reshape(dec_batch, t_new, D_MODEL)

    return (y_prompt, y_sample,
            fk.reshape(1, batch, seq, FOX_HEADS, FOX_DH), fv.reshape(1, batch, seq, FOX_HEADS, FOX_DH),
            lf8.reshape(1, batch, seq, FOX_HEADS), s_prompt[None],
            fk_s.reshape(1, dec_batch, t_new, FOX_HEADS, FOX_DH),
            fv_s.reshape(1, dec_batch, t_new, FOX_HEADS, FOX_DH),
            lf8_s.reshape(1, dec_batch, t_new, FOX_HEADS), s_sample[None])
```

```python
import functools
import math

import jax
import jax.numpy as jnp
from jax import lax
from jax.experimental import pallas as pl
from jax.experimental.pallas import tpu as pltpu

F32 = jnp.float32
BF16 = jnp.bfloat16
HIGHEST = lax.Precision.HIGHEST

D_MODEL = 1024
RET_HEADS = 4
RET_DK = 128
RET_DV = 256
FOX_HEADS = 8
FOX_DH = 128
D_FF = 4 * D_MODEL
CHUNK = 128
ROPE_BASE = 10000.0
LN_EPS = 1e-5
GN_EPS = 1e-6
DEPTH = 1
ALPHA = (2.0 * DEPTH) ** 0.25
RET_QK_W = RET_HEADS * RET_DK
RET_V_W = RET_HEADS * RET_DV
FOX_W = FOX_HEADS * FOX_DH
LOG2E = math.log2(math.e)
LANES = 128
SUBLANES = 8

NT_DIMS = (((1,), (1,)), ((), ()))
TN_DIMS = (((0,), (0,)), ((), ()))

C_RQ, C_RK, C_RV, C_RG, C_FQ, C_FK, C_FV, C_GA, C_GB, C_FF, C_END = (
    0, 512, 1024, 2048, 3072, 4096, 5120, 6144, 7168, 8192, 8320)


def _resident(shape):
    return pl.BlockSpec(shape, lambda *_: (0,) * len(shape), pipeline_mode=pl.Buffered(1))


def _log_sigmoid(z):
    return jnp.minimum(z, 0.0) - jnp.log1p(jnp.exp(-jnp.abs(z)))


def _layer_norm(z, g, b):
    mu = jnp.mean(z, axis=-1, keepdims=True)
    zc = z - mu
    var = jnp.mean(zc * zc, axis=-1, keepdims=True)
    return zc * lax.rsqrt(var + LN_EPS) * g + b


def _ret_log_decay(h):
    return math.log(1.0 - 2.0 ** (-5.0 - h))


def _inproj_kernel(x_ref, w_ref, cos_ref, sin_ref, bf_ref,
                   rq_ref, rk_ref, rv_ref, rg_ref, fq_ref, fk_ref, fv_ref,
                   ga_ref, gb_ref, lf8_ref, lfp_ref, *extra_refs):
    xb = x_ref[...].astype(BF16)

    def mm(lo, hi):
        return jnp.dot(xb, w_ref[:, lo:hi], preferred_element_type=F32)

    cos = cos_ref[...]
    sin = sin_ref[...]

    def rope(v):
        outs = []
        for h in range(RET_HEADS):
            vh = v[:, h * RET_DK:(h + 1) * RET_DK]
            outs.append(vh * cos + pltpu.roll(vh, RET_DK // 2, 1) * sin)
        return jnp.concatenate(outs, axis=1)

    rq_ref[...] = rope(mm(C_RQ, C_RK)).astype(rq_ref.dtype)
    rk_ref[...] = rope(mm(C_RK, C_RV)) * (RET_DK ** -0.5)
    rv_ref[...] = mm(C_RV, C_RG).astype(rv_ref.dtype)
    rg_ref[...] = mm(C_RG, C_FQ)
    fq_ref[...] = mm(C_FQ, C_FK).astype(fq_ref.dtype)
    fk = mm(C_FK, C_FV)
    fk_ref[...] = fk
    fv = mm(C_FV, C_GA)
    fv_ref[...] = fv
    if extra_refs:
        fkb_ref, fvt_ref = extra_refs
        fkb_ref[...] = fk.astype(BF16)
        for h in range(FOX_HEADS):
            fvt_ref[0, h, 0] = fv[:, h * FOX_DH:(h + 1) * FOX_DH].T.astype(BF16)
    ga_ref[...] = mm(C_GA, C_GB)
    gb_ref[...] = mm(C_GB, C_FF)
    lf = _log_sigmoid(mm(C_FF, C_END) + bf_ref[...])
    lfp_ref[...] = lf
    lf8_ref[...] = lf[:, :FOX_HEADS]


def _inproj(x, w_pack, cos_t, sin_t, bf_pad, *, tm, act_dtype, prompt_seq=None):
    m = x.shape[0]
    n_tab = cos_t.shape[0] // tm
    row = lambda w: pl.BlockSpec((tm, w), lambda i: (i, 0))
    tab = pl.BlockSpec((tm, LANES), lambda i: (i % n_tab, 0))
    out_shape = [
        jax.ShapeDtypeStruct((m, RET_QK_W), act_dtype),
        jax.ShapeDtypeStruct((m, RET_QK_W), F32),
        jax.ShapeDtypeStruct((m, RET_V_W), act_dtype),
        jax.ShapeDtypeStruct((m, RET_V_W), F32),
        jax.ShapeDtypeStruct((m, FOX_W), act_dtype),
        jax.ShapeDtypeStruct((m, FOX_W), F32),
        jax.ShapeDtypeStruct((m, FOX_W), F32),
        jax.ShapeDtypeStruct((m, D_MODEL), F32),
        jax.ShapeDtypeStruct((m, D_MODEL), F32),
        jax.ShapeDtypeStruct((m, FOX_HEADS), F32),
        jax.ShapeDtypeStruct((m, LANES), F32),
    ]
    out_specs = [row(RET_QK_W), row(RET_QK_W), row(RET_V_W), row(RET_V_W), row(FOX_W),
                 row(FOX_W), row(FOX_W), row(D_MODEL), row(D_MODEL), row(FOX_HEADS), row(LANES)]
    if prompt_seq is not None:
        nt = prompt_seq // tm
        out_shape += [
            jax.ShapeDtypeStruct((m, FOX_W), BF16),
            jax.ShapeDtypeStruct((m // prompt_seq, FOX_HEADS, nt, FOX_DH, tm), BF16),
        ]
        out_specs += [row(FOX_W),
                      pl.BlockSpec((1, FOX_HEADS, 1, FOX_DH, tm), lambda i: (i // nt, 0, i % nt, 0, 0))]
    return pl.pallas_call(
        _inproj_kernel,
        grid=(m // tm,),
        in_specs=[row(D_MODEL), _resident(w_pack.shape), tab, tab, _resident(bf_pad.shape)],
        out_specs=out_specs,
        out_shape=out_shape,
        compiler_params=pltpu.CompilerParams(
            dimension_semantics=("parallel",), vmem_limit_bytes=56 * 1024 * 1024),
        name="inproj",
    )(x, w_pack, cos_t, sin_t, bf_pad)


def _cumsum_kernel(lf_ref, fcol_ref, frow_ref, *, blk):
    t = lf_ref.shape[0]
    ri = lax.broadcasted_iota(jnp.int32, (blk, blk), 0)
    ci = lax.broadcasted_iota(jnp.int32, (blk, blk), 1)
    tri = (ri >= ci).astype(F32)
    carry = jnp.zeros((1, LANES), F32)
    for b in range(t // blk):
        xb = lf_ref[b * blk:(b + 1) * blk, :]
        fb = jnp.dot(tri, xb, precision=HIGHEST, preferred_element_type=F32) + carry
        fcol_ref[b * blk:(b + 1) * blk, :] = fb
        frow_ref[:, b * blk:(b + 1) * blk] = fb.T[:FOX_HEADS, :]
        carry = fb[blk - 1:blk, :]


def _cumsum(lf_pad, *, batch, seq):
    return pl.pallas_call(
        functools.partial(_cumsum_kernel, blk=LANES),
        grid=(batch,),
        in_specs=[pl.BlockSpec((seq, LANES), lambda b: (b, 0))],
        out_specs=(pl.BlockSpec((seq, LANES), lambda b: (b, 0)),
                   pl.BlockSpec((FOX_HEADS, seq), lambda b: (b, 0))),
        out_shape=(jax.ShapeDtypeStruct((batch * seq, LANES), F32),
                   jax.ShapeDtypeStruct((batch * FOX_HEADS, seq), F32)),
        compiler_params=pltpu.CompilerParams(dimension_semantics=("parallel",)),
        name="logf_cumsum",
    )(lf_pad)


def _ret_head_out(o, gain, rg):
    mu = jnp.mean(o, axis=-1, keepdims=True)
    oc = o - mu
    var = jnp.mean(oc * oc, axis=-1, keepdims=True)
    y = oc * lax.rsqrt(var + GN_EPS) * gain
    return y * (rg * jax.nn.sigmoid(rg))


def _ret_prompt_kernel(q_ref, k_ref, v_ref, rg_ref, gain_ref, o_ref, sfin_ref, s_scr, *, chunk):
    c = pl.program_id(1)

    @pl.when(c == 0)
    def _():
        s_scr[...] = jnp.zeros_like(s_scr)

    ii = lax.broadcasted_iota(jnp.int32, (chunk, chunk), 0)
    jj = lax.broadcasted_iota(jnp.int32, (chunk, chunk), 1)
    causal = ii >= jj
    expo = jnp.where(causal, (ii - jj).astype(F32), 0.0)
    ic = lax.broadcasted_iota(jnp.int32, (chunk, 1), 0).astype(F32)
    for h in range(RET_HEADS):
        ld = _ret_log_decay(h)
        dmat = jnp.where(causal, jnp.exp(expo * ld), 0.0)
        cross_decay = jnp.exp((ic + 1.0) * ld)
        k_decay = jnp.exp((chunk - 1.0 - ic) * ld)
        chunk_decay = math.exp(chunk * ld)
        s_run = s_scr[h]
        for cc in range(q_ref.shape[0] // chunk):
            rows = slice(cc * chunk, (cc + 1) * chunk)
            qh = q_ref[rows, h * RET_DK:(h + 1) * RET_DK].astype(BF16)
            kh = k_ref[rows, h * RET_DK:(h + 1) * RET_DK]
            vh = v_ref[rows, h * RET_DV:(h + 1) * RET_DV].astype(BF16)
            sc = lax.dot_general(qh, kh.astype(BF16), NT_DIMS, preferred_element_type=F32) * dmat
            intra = jnp.dot(sc.astype(BF16), vh, preferred_element_type=F32)
            cross = jnp.dot(qh, s_run.astype(BF16), preferred_element_type=F32) * cross_decay
            kd = (kh * k_decay).astype(BF16)
            s_run = s_run * chunk_decay + lax.dot_general(kd, vh, TN_DIMS, preferred_element_type=F32)
            sl = slice(h * RET_DV, (h + 1) * RET_DV)
            o_ref[rows, sl] = _ret_head_out(
                intra + cross, gain_ref[:, sl], rg_ref[rows, sl]).astype(o_ref.dtype)
        s_scr[h] = s_run

    @pl.when(c == pl.num_programs(1) - 1)
    def _():
        sfin_ref[0] = s_scr[...]


def _ret_prompt(rq, rk, rv, rg, gain, *, batch, seq, chunks_per_step):
    rows = CHUNK * chunks_per_step
    nc = seq // rows
    row = lambda w: pl.BlockSpec((rows, w), lambda b, c: (b * nc + c, 0))
    return pl.pallas_call(
        functools.partial(_ret_prompt_kernel, chunk=CHUNK),
        grid=(batch, nc),
        in_specs=[row(RET_QK_W), row(RET_QK_W), row(RET_V_W), row(RET_V_W),
                  pl.BlockSpec((1, RET_V_W), lambda b, c: (0, 0))],
        out_specs=(row(RET_V_W),
                   pl.BlockSpec((1, RET_HEADS, RET_DK, RET_DV), lambda b, c: (b, 0, 0, 0))),
        out_shape=(jax.ShapeDtypeStruct((batch * seq, RET_V_W), BF16),
                   jax.ShapeDtypeStruct((batch, RET_HEADS, RET_DK, RET_DV), F32)),
        scratch_shapes=[pltpu.VMEM((RET_HEADS, RET_DK, RET_DV), F32)],
        compiler_params=pltpu.CompilerParams(dimension_semantics=("parallel", "arbitrary")),
        name="ret_prompt",
    )(rq, rk, rv, rg, gain)


def _ret_sample_kernel(q_ref, k_ref, v_ref, rg_ref, gain_ref, s0_ref, o_ref, s1_ref, *, group, t_new):
    rows = group * t_new
    shift = t_new.bit_length() - 1
    ri = lax.broadcasted_iota(jnp.int32, (rows, rows), 0)
    ci = lax.broadcasted_iota(jnp.int32, (rows, rows), 1)
    valid = ((ri >> shift) == (ci >> shift)) & (ri >= ci)
    expo = jnp.where(valid, (ri - ci).astype(F32), 0.0)
    r1 = lax.broadcasted_iota(jnp.int32, (rows, 1), 0)
    tpos = (r1 & (t_new - 1)).astype(F32)
    rowgrp = lax.broadcasted_iota(jnp.int32, (rows, RET_DK), 0) >> shift
    for h in range(RET_HEADS):
        ld = _ret_log_decay(h)
        dmat = jnp.where(valid, jnp.exp(expo * ld), 0.0)
        cross_decay = jnp.exp((tpos + 1.0) * ld)
        k_decay = jnp.exp((t_new - 1.0 - tpos) * ld)
        chunk_decay = math.exp(t_new * ld)
        qh = q_ref[:, h * RET_DK:(h + 1) * RET_DK]
        kh = k_ref[:, h * RET_DK:(h + 1) * RET_DK]
        vh = v_ref[:, h * RET_DV:(h + 1) * RET_DV].astype(BF16)
        qb = qh.astype(BF16)
        sc = lax.dot_general(qb, kh.astype(BF16), NT_DIMS, preferred_element_type=F32) * dmat
        o = jnp.dot(sc.astype(BF16), vh, preferred_element_type=F32)
        kd = kh * k_decay
        cross = jnp.zeros((rows, RET_DV), F32)
        for g in range(group):
            s_old = s0_ref[g, h]
            in_g = rowgrp == g
            qg = jnp.where(in_g, qh, 0.0).astype(BF16)
            cross = cross + jnp.dot(qg, s_old.astype(BF16), preferred_element_type=F32)
            kg = jnp.where(in_g, kd, 0.0).astype(BF16)
            s1_ref[g, h] = s_old * chunk_decay + lax.dot_general(
                kg, vh, TN_DIMS, preferred_element_type=F32)
        o = o + cross * cross_decay
        sl = slice(h * RET_DV, (h + 1) * RET_DV)
        o_ref[:, sl] = _ret_head_out(o, gain_ref[:, sl], rg_ref[:, sl]).astype(o_ref.dtype)


def _ret_sample(rq, rk, rv, rg, gain, state, *, t_new, group):
    m = rq.shape[0]
    rows = group * t_new
    row = lambda w: pl.BlockSpec((rows, w), lambda i: (i, 0))
    st = pl.BlockSpec((group, RET_HEADS, RET_DK, RET_DV), lambda i: (i, 0, 0, 0))
    return pl.pallas_call(
        functools.partial(_ret_sample_kernel, group=group, t_new=t_new),
        grid=(m // rows,),
        in_specs=[row(RET_QK_W), row(RET_QK_W), row(RET_V_W), row(RET_V_W),
                  pl.BlockSpec((1, RET_V_W), lambda i: (0, 0)), st],
        out_specs=(row(RET_V_W), st),
        out_shape=(jax.ShapeDtypeStruct((m, RET_V_W), BF16),
                   jax.ShapeDtypeStruct(state.shape, F32)),
        compiler_params=pltpu.CompilerParams(
            dimension_semantics=("parallel",), vmem_limit_bytes=48 * 1024 * 1024),
        name="ret_sample",
    )(rq, rk, rv, rg, gain, state)


def _fox_prompt_kernel(q_ref, k_ref, vt_ref, fcum_ref, frow_ref, o_ref,
                       fk_scr, t0_scr, t1_scr, p0_scr, p1_scr, *, tile, scale):
    h = pl.program_id(1)
    i = pl.program_id(2)
    seq = k_ref.shape[0]
    vt_tile = vt_ref.shape[-1]
    per = tile // vt_tile
    assert per % 2 == 0

    @pl.when(i == 0)
    def _():
        lane = lax.broadcasted_iota(jnp.int32, (seq, LANES), 1)
        col = jnp.sum(jnp.where(lane == h, fcum_ref[...], 0.0), axis=1, keepdims=True)
        fk_scr[...] = jnp.broadcast_to(col * LOG2E, (seq, LANES))

    q = q_ref[...]
    fq2 = frow_ref[0, pl.ds(i, 1), :] * LOG2E
    t_scr = (t0_scr, t1_scr)
    p_scr = (p0_scr, p1_scr)

    def scores(u):
        start = pl.multiple_of(u * vt_tile, vt_tile)
        ks = k_ref[pl.ds(start, vt_tile), :]
        fk2 = fk_scr[pl.ds(start, vt_tile), :]
        t = lax.dot_general(ks, q, NT_DIMS, preferred_element_type=F32) * (scale * LOG2E)
        return t - jnp.concatenate([fk2] * (tile // LANES), axis=1)

    def sub_step(u, slot, state, *, diag_part=None, last=False):
        m_run, l_run, acc, alpha_prev = state
        t = t_scr[slot][...]
        if not last:
            t_scr[1 - slot][...] = scores(u + 1)
        pv_prev = jnp.dot(vt_ref[0, 0, jnp.maximum(u - 1, 0)], p_scr[1 - slot][...],
                          preferred_element_type=F32)
        acc = alpha_prev * acc + pv_prev
        if diag_part is not None:
            key = diag_part * vt_tile + lax.broadcasted_iota(jnp.int32, (vt_tile, tile), 0)
            qry = lax.broadcasted_iota(jnp.int32, (vt_tile, tile), 1)
            t = jnp.where(key <= qry, t, -jnp.inf)
        m_new = jnp.maximum(m_run, jnp.max(t, axis=0, keepdims=True) + fq2)
        alpha = jnp.exp2(m_run - m_new)
        p = jnp.exp2(t + (fq2 - m_new))
        l_new = alpha * l_run + jnp.sum(p, axis=0, keepdims=True)
        p_scr[slot][...] = p.astype(BF16)
        return m_new, l_new, acc, alpha

    def pair(j, state):
        for s in range(per):
            state = sub_step(j * per + s, s % 2, state)
        return state

    t_scr[0][...] = scores(0)
    p_scr[1][...] = jnp.zeros_like(p_scr[1])
    state = (jnp.full((1, tile), -jnp.inf, F32), jnp.zeros((1, tile), F32),
             jnp.zeros((FOX_DH, tile), F32), jnp.ones((1, tile), F32))
    state = lax.fori_loop(0, i, pair, state)
    for s in range(per):
        state = sub_step(i * per + s, s % 2, state, diag_part=s, last=(s == per - 1))
    _, l_fin, acc, alpha_prev = state
    last_u = i * per + per - 1
    acc = alpha_prev * acc + jnp.dot(vt_ref[0, 0, last_u], p_scr[(per - 1) % 2][...],
                                     preferred_element_type=F32)
    o_ref[...] = (acc / l_fin).T.astype(o_ref.dtype)


def _fox_prompt(fq, fkb, fvt, fcum, frow, *, batch, seq, tile):
    nq = seq // tile
    frow3 = frow.reshape(batch * FOX_HEADS, nq, tile)
    nt, vt_tile = fvt.shape[2], fvt.shape[4]
    return pl.pallas_call(
        functools.partial(_fox_prompt_kernel, tile=tile, scale=FOX_DH ** -0.5),
        grid=(batch, FOX_HEADS, nq),
        in_specs=[pl.BlockSpec((tile, FOX_DH), lambda b, h, i: (b * nq + i, h)),
                  pl.BlockSpec((seq, FOX_DH), lambda b, h, i: (b, h)),
                  pl.BlockSpec((1, 1, nt, FOX_DH, vt_tile), lambda b, h, i: (b, h, 0, 0, 0)),
                  pl.BlockSpec((seq, LANES), lambda b, h, i: (b, 0)),
                  pl.BlockSpec((1, nq, tile), lambda b, h, i: (b * FOX_HEADS + h, 0, 0))],
        out_specs=pl.BlockSpec((tile, FOX_DH), lambda b, h, i: (b * nq + i, h)),
        out_shape=jax.ShapeDtypeStruct((batch * seq, FOX_W), BF16),
        scratch_shapes=[pltpu.VMEM((seq, LANES), F32),
                        pltpu.VMEM((vt_tile, tile), F32), pltpu.VMEM((vt_tile, tile), F32),
                        pltpu.VMEM((vt_tile, tile), BF16), pltpu.VMEM((vt_tile, tile), BF16)],
        compiler_params=pltpu.CompilerParams(
            dimension_semantics=("parallel", "parallel", "arbitrary")),
        name="fox_prompt",
    )(fq, fkb, fvt, fcum, frow3)


def _fox_sample_kernel(pt_ref, q_ref, kn_ref, vn_ref, ln_ref, *refs, n_pages, scale, page_cols):
    del pt_ref
    kc_refs = refs[:n_pages]
    vc_refs = refs[n_pages:2 * n_pages]
    lc_refs = refs[2 * n_pages:3 * n_pages]
    o_ref, s_scr = refs[3 * n_pages:]
    t_new = q_ref.shape[0]
    assert t_new == SUBLANES
    n_rows = FOX_HEADS * t_new
    tshift = t_new.bit_length() - 1
    hmask = FOX_HEADS - 1
    li = lax.broadcasted_iota(jnp.int32, (LANES, LANES), 0)
    lj = lax.broadcasted_iota(jnp.int32, (LANES, LANES), 1)
    same_head = ((li - lj) & hmask) == 0

    def stack_heads(ref):
        return jnp.concatenate([ref[:, h * FOX_DH:(h + 1) * FOX_DH] for h in range(FOX_HEADS)], axis=0)

    prefix = (((li >> tshift) == (lj >> tshift)) & (li <= lj)).astype(F32)
    xn = jnp.broadcast_to(ln_ref[0], (SUBLANES, LANES))
    fn_row = jnp.dot(xn, prefix, precision=HIGHEST, preferred_element_type=F32)[0:1, :]
    eye = (lax.broadcasted_iota(jnp.int32, (n_rows, LANES), 0)
           == lax.broadcasted_iota(jnp.int32, (n_rows, LANES), 1))
    fq = jnp.sum(jnp.where(eye, fn_row, 0.0), axis=1, keepdims=True)

    assert n_pages * lc_refs[0].shape[1] == LANES
    x = jnp.concatenate([lc_refs[pg][0] for pg in range(n_pages)], axis=0)
    within = jnp.dot(x, (same_head & (li > lj)).astype(F32), precision=HIGHEST,
                     preferred_element_type=F32)
    tot = jnp.dot(x, same_head.astype(F32), precision=HIGHEST, preferred_element_type=F32)
    later_rows = jnp.dot((lj > li).astype(F32), tot, precision=HIGHEST, preferred_element_type=F32)
    g2 = (within + later_rows) * (-LOG2E)
    rows_per_page = lc_refs[0].shape[1]
    g_rows = [jnp.concatenate([g2[pg * rows_per_page + r:pg * rows_per_page + r + 1, :]
                               for r in range(rows_per_page)], axis=1)
              for pg in range(n_pages)]

    c_qk = scale * LOG2E
    fq2 = fq * LOG2E
    qb = stack_heads(q_ref).astype(BF16)
    ri = lax.broadcasted_iota(jnp.int32, (n_rows, page_cols), 0)
    ci = lax.broadcasted_iota(jnp.int32, (n_rows, page_cols), 1)
    head_ok = (ri >> tshift) == (ci & hmask)
    mx = jnp.full((n_rows, 1), -jnp.inf, F32)
    for pg in range(n_pages):
        s = lax.dot_general(qb, kc_refs[pg][0].astype(BF16), NT_DIMS, preferred_element_type=F32)
        s = jnp.where(head_ok, s * c_qk - g_rows[pg], -jnp.inf)
        s_scr[pg] = s
        mx = jnp.maximum(mx, jnp.max(s, axis=1, keepdims=True))

    zpad = jnp.zeros((LANES - n_rows, FOX_DH), BF16)
    kn = jnp.concatenate([stack_heads(kn_ref).astype(BF16), zpad], axis=0)
    vn = jnp.concatenate([stack_heads(vn_ref).astype(BF16), zpad], axis=0)
    r2 = lax.broadcasted_iota(jnp.int32, (n_rows, LANES), 0)
    c2 = lax.broadcasted_iota(jnp.int32, (n_rows, LANES), 1)
    ok = ((r2 >> tshift) == (c2 >> tshift)) & (c2 <= r2)
    s2 = lax.dot_general(qb, kn, NT_DIMS, preferred_element_type=F32)
    s2 = jnp.where(ok, s2 * c_qk - fn_row * LOG2E, -jnp.inf)
    mx = jnp.maximum(mx, jnp.max(s2, axis=1, keepdims=True))

    shift = fq2 - (mx + fq2)
    l = jnp.zeros((n_rows, 1), F32)
    acc = jnp.zeros((n_rows, FOX_DH), F32)
    for pg in range(n_pages):
        pr = jnp.exp2(s_scr[pg] + shift)
        l = l + jnp.sum(pr, axis=1, keepdims=True)
        acc = acc + jnp.dot(pr.astype(BF16), vc_refs[pg][0].astype(BF16), preferred_element_type=F32)
    p2 = jnp.exp2(s2 + shift)
    l = l + jnp.sum(p2, axis=1, keepdims=True)
    acc = acc + jnp.dot(p2.astype(BF16), vn, preferred_element_type=F32)
    out = acc / l
    for h in range(FOX_HEADS):
        o_ref[:, h * FOX_DH:(h + 1) * FOX_DH] = out[h * t_new:(h + 1) * t_new, :]


def _page_index(b, pt, *, j):
    return (pt[b, j], 0, 0)


def _fox_sample(page_table, fq, fk, fv, lf_new, cache_k, cache_v, cache_lf, *, t_new):
    dec_batch, n_pages = page_table.shape
    n_phys, page_size = cache_k.shape[0], cache_k.shape[1]
    n_rows = t_new * FOX_HEADS
    page_cols = page_size * FOX_HEADS
    ln = lf_new.reshape(dec_batch, t_new, FOX_HEADS).transpose(0, 2, 1).reshape(dec_batch, 1, n_rows)
    ln = jnp.pad(ln, ((0, 0), (0, 0), (0, LANES - n_rows)))
    kc = cache_k.reshape(n_phys, page_cols, FOX_DH)
    vc = cache_v.reshape(n_phys, page_cols, FOX_DH)
    lc = cache_lf.reshape(n_phys, page_cols // LANES, LANES)
    new_spec = pl.BlockSpec((t_new, FOX_W), lambda b, pt: (b, 0))
    kv_specs = [pl.BlockSpec((1, page_cols, FOX_DH), functools.partial(_page_index, j=j))
                for j in range(n_pages)]
    lf_specs = [pl.BlockSpec((1, page_cols // LANES, LANES), functools.partial(_page_index, j=j))
                for j in range(n_pages)]
    grid_spec = pltpu.PrefetchScalarGridSpec(
        num_scalar_prefetch=1,
        grid=(dec_batch,),
        in_specs=[new_spec, new_spec, new_spec, pl.BlockSpec((1, 1, LANES), lambda b, pt: (b, 0, 0))]
        + kv_specs + kv_specs + lf_specs,
        out_specs=new_spec,
        scratch_shapes=[pltpu.VMEM((n_pages, n_rows, page_cols), F32)],
    )
    return pl.pallas_call(
        functools.partial(_fox_sample_kernel, n_pages=n_pages, scale=FOX_DH ** -0.5,
                          page_cols=page_cols),
        grid_spec=grid_spec,
        out_shape=jax.ShapeDtypeStruct((dec_batch * t_new, FOX_W), F32),
        compiler_params=pltpu.CompilerParams(
            dimension_semantics=("parallel",), vmem_limit_bytes=56 * 1024 * 1024),
        name="fox_sample",
    )(page_table, fq, fk, fv, ln, *([kc] * n_pages), *([vc] * n_pages), *([lc] * n_pages))


def _merge_kernel(ro_ref, fo_ref, ga_ref, gb_ref, x_ref, wr_ref, wf_ref, wo_ref, g_ref, b_ref, h_ref):
    a = jnp.dot(ro_ref[...].astype(BF16), wr_ref[...], preferred_element_type=F32)
    b = jnp.dot(fo_ref[...].astype(BF16), wf_ref[...], preferred_element_type=F32)
    merged = jax.nn.sigmoid(ga_ref[...]) * a + jax.nn.sigmoid(gb_ref[...]) * b
    mix = jnp.dot(merged.astype(BF16), wo_ref[...], preferred_element_type=F32)
    h_ref[...] = _layer_norm(ALPHA * x_ref[...] + mix, g_ref[...], b_ref[...])


def _merge(ret_o, fox_o, ga, gb, x, wr, wf, wo, g, b, *, tm):
    m = x.shape[0]
    row = lambda w: pl.BlockSpec((tm, w), lambda i: (i, 0))
    return pl.pallas_call(
        _merge_kernel,
        grid=(m // tm,),
        in_specs=[row(RET_V_W), row(FOX_W), row(D_MODEL), row(D_MODEL), row(D_MODEL),
                  _resident(wr.shape), _resident(wf.shape), _resident(wo.shape),
                  _resident(g.shape), _resident(b.shape)],
        out_specs=row(D_MODEL),
        out_shape=jax.ShapeDtypeStruct((m, D_MODEL), F32),
        compiler_params=pltpu.CompilerParams(
            dimension_semantics=("parallel",), vmem_limit_bytes=48 * 1024 * 1024),
        name="merge_out_ln",
    )(ret_o, fox_o, ga, gb, x, wr, wf, wo, g, b)


def _ffn_kernel(h_ref, wu_ref, wd_ref, g_ref, b_ref, y_ref, *, ff_tile):
    h = h_ref[...]
    hb = h.astype(BF16)
    acc = jnp.zeros(h.shape, F32)
    for c in range(D_FF // ff_tile):
        sl = slice(c * ff_tile, (c + 1) * ff_tile)
        u = jnp.maximum(jnp.dot(hb, wu_ref[:, sl], preferred_element_type=F32), 0.0)
        acc = acc + jnp.dot((u * u).astype(BF16), wd_ref[sl, :], preferred_element_type=F32)
    y_ref[...] = _layer_norm(ALPHA * h + acc, g_ref[...], b_ref[...])


def _ffn(h, wu, wd, g, b, *, tm, ff_tile):
    m = h.shape[0]
    row = pl.BlockSpec((tm, D_MODEL), lambda i: (i, 0))
    return pl.pallas_call(
        functools.partial(_ffn_kernel, ff_tile=ff_tile),
        grid=(m // tm,),
        in_specs=[row, _resident(wu.shape), _resident(wd.shape), _resident(g.shape),
                  _resident(b.shape)],
        out_specs=row,
        out_shape=jax.ShapeDtypeStruct((m, D_MODEL), F32),
        compiler_params=pltpu.CompilerParams(
            dimension_semantics=("parallel",), vmem_limit_bytes=48 * 1024 * 1024),
        name="ffn_ln",
    )(h, wu, wd, g, b)


def _rope_tables(pos):
    half = RET_DK // 2
    inv_freq = ROPE_BASE ** (-jnp.arange(half, dtype=F32) / half)
    ang = pos.astype(F32)[:, None] * inv_freq[None, :]
    c, s = jnp.cos(ang), jnp.sin(ang)
    return jnp.concatenate([c, c], axis=1), jnp.concatenate([-s, s], axis=1)


def kernel(x_prompt, x_sample, cache_k, cache_v, cache_logf, state_ret, page_table, w_in, b_forget,
           ret_gn_gain, w_ret_proj, w_fox_proj, w_out, ln1_g, ln1_b, w_ff_up, w_ff_down, ln2_g,
           ln2_b):
    batch, seq, _ = x_prompt.shape
    dec_batch, t_new, _ = x_sample.shape
    page_size = cache_k.shape[2]
    past_len = page_table.shape[1] * page_size
    tm = 256

    w = w_in[0]
    ff_lo, ff_hi = 2 * RET_QK_W + 2 * RET_V_W + 3 * FOX_W, 2 * RET_QK_W + 2 * RET_V_W + 3 * FOX_W + FOX_HEADS
    w_pack = jnp.concatenate(
        [w[:, :ff_lo], w[:, ff_hi:], jnp.pad(w[:, ff_lo:ff_hi], ((0, 0), (0, LANES - FOX_HEADS)))],
        axis=1).astype(BF16)
    bf_pad = jnp.pad(b_forget[0].astype(F32), (0, LANES - FOX_HEADS)).reshape(1, LANES)
    gain = ret_gn_gain[0].reshape(1, RET_V_W)
    wr, wf, wo = w_ret_proj[0].astype(BF16), w_fox_proj[0].astype(BF16), w_out[0].astype(BF16)
    wu, wd = w_ff_up[0].astype(BF16), w_ff_down[0].astype(BF16)
    g1, b1 = ln1_g[0].reshape(1, D_MODEL), ln1_b[0].reshape(1, D_MODEL)
    g2, b2 = ln2_g[0].reshape(1, D_MODEL), ln2_b[0].reshape(1, D_MODEL)

    def tail(ret_o, fox_o, ga, gb, x2):
        h = _merge(ret_o, fox_o, ga, gb, x2, wr, wf, wo, g1, b1, tm=tm)
        return _ffn(h, wu, wd, g2, b2, tm=tm, ff_tile=512)

    xp = x_prompt.reshape(batch * seq, D_MODEL)
    cos_p, sin_p = _rope_tables(jnp.arange(seq, dtype=jnp.int32))
    (rq, rk, rv, rg, fq, fk, fv, ga, gb, lf8, lfp, fkb, fvt) = _inproj(
        xp, w_pack, cos_p, sin_p, bf_pad, tm=tm, act_dtype=BF16, prompt_seq=seq)
    fcum, frow = _cumsum(lfp, batch=batch, seq=seq)
    ret_o, s_prompt = _ret_prompt(rq, rk, rv, rg, gain, batch=batch, seq=seq, chunks_per_step=4)
    fox_o = _fox_prompt(fq, fkb, fvt, fcum, frow, batch=batch, seq=seq, tile=512)
    y_prompt = tail(ret_o, fox_o, ga, gb, xp).reshape(batch, seq, D_MODEL)

    xs = x_sample.reshape(dec_batch * t_new, D_MODEL)
    pos_s = past_len + (jnp.arange(dec_batch * t_new, dtype=jnp.int32) % t_new)
    cos_s, sin_s = _rope_tables(pos_s)
    (rq_s, rk_s, rv_s, rg_s, fq_s, fk_s, fv_s, ga_s, gb_s, lf8_s, _) = _inproj(
        xs, w_pack, cos_s, sin_s, bf_pad, tm=tm, act_dtype=F32)
    ret_o_s, s_sample = _ret_sample(rq_s, rk_s, rv_s, rg_s, gain, state_ret[0], t_new=t_new, group=16)
    fox_o_s = _fox_sample(page_table, fq_s, fk_s, fv_s, lf8_s, cache_k[0], cache_v[0], cache_logf[0],
                          t_new=t_new)
    y_sample = tail(ret_o_s, fox_o_s, ga_s, gb_s, xs).reshape(dec_batch, t_new, D_MODEL)

    return (y_prompt, y_sample,
            fk.reshape(1, batch, seq, FOX_HEADS, FOX_DH), fv.reshape(1, batch, seq, FOX_HEADS, FOX_DH),
            lf8.reshape(1, batch, seq, FOX_HEADS), s_prompt[None],
            fk_s.reshape(1, dec_batch, t_new, FOX_HEADS, FOX_DH),
            fv_s.reshape(1, dec_batch, t_new, FOX_HEADS, FOX_DH),
            lf8_s.reshape(1, dec_batch, t_new, FOX_HEADS), s_sample[None])
```

```python
import functools
import math

import jax
import jax.numpy as jnp
from jax import lax
from jax.experimental import pallas as pl
from jax.experimental.pallas import tpu as pltpu

F32 = jnp.float32
BF16 = jnp.bfloat16
HIGHEST = lax.Precision.HIGHEST

D_MODEL = 1024
RET_HEADS = 4
RET_DK = 128
RET_DV = 256
FOX_HEADS = 8
FOX_DH = 128
D_FF = 4 * D_MODEL
CHUNK = 128
ROPE_BASE = 10000.0
LN_EPS = 1e-5
GN_EPS = 1e-6
DEPTH = 1
ALPHA = (2.0 * DEPTH) ** 0.25
RET_QK_W = RET_HEADS * RET_DK
RET_V_W = RET_HEADS * RET_DV
FOX_W = FOX_HEADS * FOX_DH
LOG2E = math.log2(math.e)
LANES = 128
SUBLANES = 8

NT_DIMS = (((1,), (1,)), ((), ()))
TN_DIMS = (((0,), (0,)), ((), ()))

C_RQ, C_RK, C_RV, C_RG, C_FQ, C_FK, C_FV, C_GA, C_GB, C_FF, C_END = (
    0, 512, 1024, 2048, 3072, 4096, 5120, 6144, 7168, 8192, 8320)


def _resident(shape):
    return pl.BlockSpec(shape, lambda *_: (0,) * len(shape), pipeline_mode=pl.Buffered(1))


def _log_sigmoid(z):
    return jnp.minimum(z, 0.0) - jnp.log1p(jnp.exp(-jnp.abs(z)))


def _layer_norm(z, g, b):
    mu = jnp.mean(z, axis=-1, keepdims=True)
    zc = z - mu
    var = jnp.mean(zc * zc, axis=-1, keepdims=True)
    return zc * lax.rsqrt(var + LN_EPS) * g + b


def _ret_log_decay(h):
    return math.log(1.0 - 2.0 ** (-5.0 - h))


def _inproj_kernel(x_ref, w_ref, cos_ref, sin_ref, bf_ref,
                   rq_ref, rk_ref, rv_ref, rg_ref, fq_ref, fk_ref, fv_ref,
                   ga_ref, gb_ref, lf8_ref, lfp_ref, *extra_refs):
    xb = x_ref[...].astype(BF16)

    def mm(lo, hi):
        return jnp.dot(xb, w_ref[:, lo:hi], preferred_element_type=F32)

    cos = cos_ref[...]
    sin = sin_ref[...]

    def rope(v):
        outs = []
        for h in range(RET_HEADS):
            vh = v[:, h * RET_DK:(h + 1) * RET_DK]
            outs.append(vh * cos + pltpu.roll(vh, RET_DK // 2, 1) * sin)
        return jnp.concatenate(outs, axis=1)

    rq_ref[...] = rope(mm(C_RQ, C_RK)).astype(rq_ref.dtype)
    rk_ref[...] = rope(mm(C_RK, C_RV)) * (RET_DK ** -0.5)
    rv_ref[...] = mm(C_RV, C_RG).astype(rv_ref.dtype)
    rg_ref[...] = mm(C_RG, C_FQ)
    fq_ref[...] = mm(C_FQ, C_FK).astype(fq_ref.dtype)
    fk = mm(C_FK, C_FV)
    fk_ref[...] = fk
    fv = mm(C_FV, C_GA)
    fv_ref[...] = fv
    if extra_refs:
        fkb_ref, fvt_ref = extra_refs
        fkb_ref[...] = fk.astype(BF16)
        for h in range(FOX_HEADS):
            fvt_ref[0, h, 0] = fv[:, h * FOX_DH:(h + 1) * FOX_DH].T.astype(BF16)
    ga_ref[...] = mm(C_GA, C_GB)
    gb_ref[...] = mm(C_GB, C_FF)
    lf = _log_sigmoid(mm(C_FF, C_END) + bf_ref[...])
    lfp_ref[...] = lf
    lf8_ref[...] = lf[:, :FOX_HEADS]


def _inproj(x, w_pack, cos_t, sin_t, bf_pad, *, tm, act_dtype, prompt_seq=None):
    m = x.shape[0]
    n_tab = cos_t.shape[0] // tm
    row = lambda w: pl.BlockSpec((tm, w), lambda i: (i, 0))
    tab = pl.BlockSpec((tm, LANES), lambda i: (i % n_tab, 0))
    out_shape = [
        jax.ShapeDtypeStruct((m, RET_QK_W), act_dtype),
        jax.ShapeDtypeStruct((m, RET_QK_W), F32),
        jax.ShapeDtypeStruct((m, RET_V_W), act_dtype),
        jax.ShapeDtypeStruct((m, RET_V_W), F32),
        jax.ShapeDtypeStruct((m, FOX_W), act_dtype),
        jax.ShapeDtypeStruct((m, FOX_W), F32),
        jax.ShapeDtypeStruct((m, FOX_W), F32),
        jax.ShapeDtypeStruct((m, D_MODEL), F32),
        jax.ShapeDtypeStruct((m, D_MODEL), F32),
        jax.ShapeDtypeStruct((m, FOX_HEADS), F32),
        jax.ShapeDtypeStruct((m, LANES), F32),
    ]
    out_specs = [row(RET_QK_W), row(RET_QK_W), row(RET_V_W), row(RET_V_W), row(FOX_W),
                 row(FOX_W), row(FOX_W), row(D_MODEL), row(D_MODEL), row(FOX_HEADS), row(LANES)]
    if prompt_seq is not None:
        nt = prompt_seq // tm
        out_shape += [
            jax.ShapeDtypeStruct((m, FOX_W), BF16),
            jax.ShapeDtypeStruct((m // prompt_seq, FOX_HEADS, nt, FOX_DH, tm), BF16),
        ]
        out_specs += [row(FOX_W),
                      pl.BlockSpec((1, FOX_HEADS, 1, FOX_DH, tm), lambda i: (i // nt, 0, i % nt, 0, 0))]
    return pl.pallas_call(
        _inproj_kernel,
        grid=(m // tm,),
        in_specs=[row(D_MODEL), _resident(w_pack.shape), tab, tab, _resident(bf_pad.shape)],
        out_specs=out_specs,
        out_shape=out_shape,
        compiler_params=pltpu.CompilerParams(
            dimension_semantics=("parallel",), vmem_limit_bytes=56 * 1024 * 1024),
        name="inproj",
    )(x, w_pack, cos_t, sin_t, bf_pad)


def _cumsum_kernel(lf_ref, fcol_ref, frow_ref, *, blk):
    t = lf_ref.shape[0]
    ri = lax.broadcasted_iota(jnp.int32, (blk, blk), 0)
    ci = lax.broadcasted_iota(jnp.int32, (blk, blk), 1)
    tri = (ri >= ci).astype(F32)
    carry = jnp.zeros((1, LANES), F32)
    for b in range(t // blk):
        xb = lf_ref[b * blk:(b + 1) * blk, :]
        fb = jnp.dot(tri, xb, precision=HIGHEST, preferred_element_type=F32) + carry
        fcol_ref[b * blk:(b + 1) * blk, :] = fb
        frow_ref[:, b * blk:(b + 1) * blk] = fb.T[:FOX_HEADS, :]
        carry = fb[blk - 1:blk, :]


def _cumsum(lf_pad, *, batch, seq):
    return pl.pallas_call(
        functools.partial(_cumsum_kernel, blk=LANES),
        grid=(batch,),
        in_specs=[pl.BlockSpec((seq, LANES), lambda b: (b, 0))],
        out_specs=(pl.BlockSpec((seq, LANES), lambda b: (b, 0)),
                   pl.BlockSpec((FOX_HEADS, seq), lambda b: (b, 0))),
        out_shape=(jax.ShapeDtypeStruct((batch * seq, LANES), F32),
                   jax.ShapeDtypeStruct((batch * FOX_HEADS, seq), F32)),
        compiler_params=pltpu.CompilerParams(dimension_semantics=("parallel",)),
        name="logf_cumsum",
    )(lf_pad)


def _ret_head_out(o, gain, rg):
    mu = jnp.mean(o, axis=-1, keepdims=True)
    oc = o - mu
    var = jnp.mean(oc * oc, axis=-1, keepdims=True)
    y = oc * lax.rsqrt(var + GN_EPS) * gain
    return y * (rg * jax.nn.sigmoid(rg))


def _ret_prompt_kernel(q_ref, k_ref, v_ref, rg_ref, gain_ref, o_ref, sfin_ref, s_scr, *, chunk):
    c = pl.program_id(1)

    @pl.when(c == 0)
    def _():
        s_scr[...] = jnp.zeros_like(s_scr)

    ii = lax.broadcasted_iota(jnp.int32, (chunk, chunk), 0)
    jj = lax.broadcasted_iota(jnp.int32, (chunk, chunk), 1)
    causal = ii >= jj
    expo = jnp.where(causal, (ii - jj).astype(F32), 0.0)
    ic = lax.broadcasted_iota(jnp.int32, (chunk, 1), 0).astype(F32)
    for h in range(RET_HEADS):
        ld = _ret_log_decay(h)
        dmat = jnp.where(causal, jnp.exp(expo * ld), 0.0)
        cross_decay = jnp.exp((ic + 1.0) * ld)
        k_decay = jnp.exp((chunk - 1.0 - ic) * ld)
        chunk_decay = math.exp(chunk * ld)
        s_run = s_scr[h]
        for cc in range(q_ref.shape[0] // chunk):
            rows = slice(cc * chunk, (cc + 1) * chunk)
            qh = q_ref[rows, h * RET_DK:(h + 1) * RET_DK].astype(BF16)
            kh = k_ref[rows, h * RET_DK:(h + 1) * RET_DK]
            vh = v_ref[rows, h * RET_DV:(h + 1) * RET_DV].astype(BF16)
            sc = lax.dot_general(qh, kh.astype(BF16), NT_DIMS, preferred_element_type=F32) * dmat
            intra = jnp.dot(sc.astype(BF16), vh, preferred_element_type=F32)
            cross = jnp.dot(qh, s_run.astype(BF16), preferred_element_type=F32) * cross_decay
            kd = (kh * k_decay).astype(BF16)
            s_run = s_run * chunk_decay + lax.dot_general(kd, vh, TN_DIMS, preferred_element_type=F32)
            sl = slice(h * RET_DV, (h + 1) * RET_DV)
            o_ref[rows, sl] = _ret_head_out(
                intra + cross, gain_ref[:, sl], rg_ref[rows, sl]).astype(o_ref.dtype)
        s_scr[h] = s_run

    @pl.when(c == pl.num_programs(1) - 1)
    def _():
        sfin_ref[0] = s_scr[...]


def _ret_prompt(rq, rk, rv, rg, gain, *, batch, seq, chunks_per_step):
    rows = CHUNK * chunks_per_step
    nc = seq // rows
    row = lambda w: pl.BlockSpec((rows, w), lambda b, c: (b * nc + c, 0))
    return pl.pallas_call(
        functools.partial(_ret_prompt_kernel, chunk=CHUNK),
        grid=(batch, nc),
        in_specs=[row(RET_QK_W), row(RET_QK_W), row(RET_V_W), row(RET_V_W),
                  pl.BlockSpec((1, RET_V_W), lambda b, c: (0, 0))],
        out_specs=(row(RET_V_W),
                   pl.BlockSpec((1, RET_HEADS, RET_DK, RET_DV), lambda b, c: (b, 0, 0, 0))),
        out_shape=(jax.ShapeDtypeStruct((batch * seq, RET_V_W), BF16),
                   jax.ShapeDtypeStruct((batch, RET_HEADS, RET_DK, RET_DV), F32)),
        scratch_shapes=[pltpu.VMEM((RET_HEADS, RET_DK, RET_DV), F32)],
        compiler_params=pltpu.CompilerParams(dimension_semantics=("parallel", "arbitrary")),
        name="ret_prompt",
    )(rq, rk, rv, rg, gain)


def _ret_sample_kernel(q_ref, k_ref, v_ref, rg_ref, gain_ref, s0_ref, o_ref, s1_ref, *, group, t_new):
    rows = group * t_new
    shift = t_new.bit_length() - 1
    ri = lax.broadcasted_iota(jnp.int32, (rows, rows), 0)
    ci = lax.broadcasted_iota(jnp.int32, (rows, rows), 1)
    valid = ((ri >> shift) == (ci >> shift)) & (ri >= ci)
    expo = jnp.where(valid, (ri - ci).astype(F32), 0.0)
    r1 = lax.broadcasted_iota(jnp.int32, (rows, 1), 0)
    tpos = (r1 & (t_new - 1)).astype(F32)
    rowgrp = lax.broadcasted_iota(jnp.int32, (rows, RET_DK), 0) >> shift
    for h in range(RET_HEADS):
        ld = _ret_log_decay(h)
        dmat = jnp.where(valid, jnp.exp(expo * ld), 0.0)
        cross_decay = jnp.exp((tpos + 1.0) * ld)
        k_decay = jnp.exp((t_new - 1.0 - tpos) * ld)
        chunk_decay = math.exp(t_new * ld)
        qh = q_ref[:, h * RET_DK:(h + 1) * RET_DK]
        kh = k_ref[:, h * RET_DK:(h + 1) * RET_DK]
        vh = v_ref[:, h * RET_DV:(h + 1) * RET_DV].astype(BF16)
        qb = qh.astype(BF16)
        sc = lax.dot_general(qb, kh.astype(BF16), NT_DIMS, preferred_element_type=F32) * dmat
        o = jnp.dot(sc.astype(BF16), vh, preferred_element_type=F32)
        kd = kh * k_decay
        cross = jnp.zeros((rows, RET_DV), F32)
        for g in range(group):
            s_old = s0_ref[g, h]
            in_g = rowgrp == g
            qg = jnp.where(in_g, qh, 0.0).astype(BF16)
            cross = cross + jnp.dot(qg, s_old.astype(BF16), preferred_element_type=F32)
            kg = jnp.where(in_g, kd, 0.0).astype(BF16)
            s1_ref[g, h] = s_old * chunk_decay + lax.dot_general(
                kg, vh, TN_DIMS, preferred_element_type=F32)
        o = o + cross * cross_decay
        sl = slice(h * RET_DV, (h + 1) * RET_DV)
        o_ref[:, sl] = _ret_head_out(o, gain_ref[:, sl], rg_ref[:, sl]).astype(o_ref.dtype)


def _ret_sample(rq, rk, rv, rg, gain, state, *, t_new, group):
    m = rq.shape[0]
    rows = group * t_new
    row = lambda w: pl.BlockSpec((rows, w), lambda i: (i, 0))
    st = pl.BlockSpec((group, RET_HEADS, RET_DK, RET_DV), lambda i: (i, 0, 0, 0))
    return pl.pallas_call(
        functools.partial(_ret_sample_kernel, group=group, t_new=t_new),
        grid=(m // rows,),
        in_specs=[row(RET_QK_W), row(RET_QK_W), row(RET_V_W), row(RET_V_W),
                  pl.BlockSpec((1, RET_V_W), lambda i: (0, 0)), st],
        out_specs=(row(RET_V_W), st),
        out_shape=(jax.ShapeDtypeStruct((m, RET_V_W), BF16),
                   jax.ShapeDtypeStruct(state.shape, F32)),
        compiler_params=pltpu.CompilerParams(
            dimension_semantics=("parallel",), vmem_limit_bytes=48 * 1024 * 1024),
        name="ret_sample",
    )(rq, rk, rv, rg, gain, state)


def _fox_prompt_kernel(q_ref, k_ref, vt_ref, fcum_ref, frow_ref, o_ref,
                       fk_scr, t_scr, p_scr, *, tile, scale):
    hps = vt_ref.shape[1]
    h0 = pl.program_id(1) * hps
    i = pl.program_id(2)
    seq = k_ref.shape[0]
    vt_tile = vt_ref.shape[-1]
    per = tile // vt_tile
    assert per % 2 == 0

    @pl.when(i == 0)
    def _():
        lane = lax.broadcasted_iota(jnp.int32, (seq, LANES), 1)
        for g in range(hps):
            col = jnp.sum(jnp.where(lane == h0 + g, fcum_ref[...], 0.0), axis=1, keepdims=True)
            fk_scr[g] = jnp.broadcast_to(col * LOG2E, (seq, LANES))

    def scores(g, u):
        start = pl.multiple_of(u * vt_tile, vt_tile)
        ks = k_ref[pl.ds(start, vt_tile), g * FOX_DH:(g + 1) * FOX_DH]
        fk2 = fk_scr[g, pl.ds(start, vt_tile), :]
        t = lax.dot_general(ks, q_ref[:, g * FOX_DH:(g + 1) * FOX_DH], NT_DIMS,
                            preferred_element_type=F32) * (scale * LOG2E)
        return t - jnp.concatenate([fk2] * (tile // LANES), axis=1)

    def sub_step(g, u, slot, state, *, diag_part=None, last=False):
        m_run, l_run, acc, alpha_prev = state
        fq2 = frow_ref[g, pl.ds(i, 1), :] * LOG2E
        if not last:
            t_scr[g, 1 - slot] = scores(g, u + 1)
        pv_prev = jnp.dot(vt_ref[0, g, jnp.maximum(u - 1, 0)], p_scr[g, 1 - slot],
                          preferred_element_type=F32)
        acc = alpha_prev * acc + pv_prev
        t = t_scr[g, slot]
        if diag_part is not None:
            key = diag_part * vt_tile + lax.broadcasted_iota(jnp.int32, (vt_tile, tile), 0)
            qry = lax.broadcasted_iota(jnp.int32, (vt_tile, tile), 1)
            t = jnp.where(key <= qry, t, -jnp.inf)
        m_new = jnp.maximum(m_run, jnp.max(t, axis=0, keepdims=True) + fq2)
        alpha = jnp.exp2(m_run - m_new)
        p = jnp.exp2(t + (fq2 - m_new))
        l_new = alpha * l_run + jnp.sum(p, axis=0, keepdims=True)
        p_scr[g, slot] = p.astype(BF16)
        return m_new, l_new, acc, alpha

    def pair(j, states):
        for s in range(per):
            states = tuple(sub_step(g, j * per + s, s % 2, states[g]) for g in range(hps))
        return states

    for g in range(hps):
        t_scr[g, 0] = scores(g, 0)
        p_scr[g, 1] = jnp.zeros((vt_tile, tile), BF16)
    init = (jnp.full((1, tile), -jnp.inf, F32), jnp.zeros((1, tile), F32),
            jnp.zeros((FOX_DH, tile), F32), jnp.ones((1, tile), F32))
    states = lax.fori_loop(0, i, pair, (init,) * hps)
    for s in range(per):
        states = tuple(sub_step(g, i * per + s, s % 2, states[g], diag_part=s, last=(s == per - 1))
                       for g in range(hps))
    last_u = i * per + per - 1
    for g in range(hps):
        _, l_fin, acc, alpha_prev = states[g]
        acc = alpha_prev * acc + jnp.dot(vt_ref[0, g, last_u], p_scr[g, (per - 1) % 2],
                                         preferred_element_type=F32)
        o_ref[:, g * FOX_DH:(g + 1) * FOX_DH] = (acc / l_fin).T.astype(o_ref.dtype)


def _fox_prompt(fq, fkb, fvt, fcum, frow, *, batch, seq, tile, heads_per_step):
    nq = seq // tile
    hps = heads_per_step
    frow3 = frow.reshape(batch * FOX_HEADS, nq, tile)
    nt, vt_tile = fvt.shape[2], fvt.shape[4]
    return pl.pallas_call(
        functools.partial(_fox_prompt_kernel, tile=tile, scale=FOX_DH ** -0.5),
        grid=(batch, FOX_HEADS // hps, nq),
        in_specs=[pl.BlockSpec((tile, hps * FOX_DH), lambda b, h, i: (b * nq + i, h)),
                  pl.BlockSpec((seq, hps * FOX_DH), lambda b, h, i: (b, h)),
                  pl.BlockSpec((1, hps, nt, FOX_DH, vt_tile), lambda b, h, i: (b, h, 0, 0, 0)),
                  pl.BlockSpec((seq, LANES), lambda b, h, i: (b, 0)),
                  pl.BlockSpec((hps, nq, tile), lambda b, h, i: (b * (FOX_HEADS // hps) + h, 0, 0))],
        out_specs=pl.BlockSpec((tile, hps * FOX_DH), lambda b, h, i: (b * nq + i, h)),
        out_shape=jax.ShapeDtypeStruct((batch * seq, FOX_W), BF16),
        scratch_shapes=[pltpu.VMEM((hps, seq, LANES), F32),
                        pltpu.VMEM((hps, 2, vt_tile, tile), F32),
                        pltpu.VMEM((hps, 2, vt_tile, tile), BF16)],
        compiler_params=pltpu.CompilerParams(
            dimension_semantics=("parallel", "parallel", "arbitrary")),
        name="fox_prompt",
    )(fq, fkb, fvt, fcum, frow3)


def _fox_sample_kernel(pt_ref, q_ref, kn_ref, vn_ref, ln_ref, *refs, n_pages, scale, page_cols):
    del pt_ref
    kc_refs = refs[:n_pages]
    vc_refs = refs[n_pages:2 * n_pages]
    lc_refs = refs[2 * n_pages:3 * n_pages]
    o_ref, s_scr = refs[3 * n_pages:]
    t_new = q_ref.shape[0]
    assert t_new == SUBLANES
    n_rows = FOX_HEADS * t_new
    tshift = t_new.bit_length() - 1
    hmask = FOX_HEADS - 1
    li = lax.broadcasted_iota(jnp.int32, (LANES, LANES), 0)
    lj = lax.broadcasted_iota(jnp.int32, (LANES, LANES), 1)
    same_head = ((li - lj) & hmask) == 0

    def stack_heads(ref):
        return jnp.concatenate([ref[:, h * FOX_DH:(h + 1) * FOX_DH] for h in range(FOX_HEADS)], axis=0)

    prefix = (((li >> tshift) == (lj >> tshift)) & (li <= lj)).astype(F32)
    xn = jnp.broadcast_to(ln_ref[0], (SUBLANES, LANES))
    fn_row = jnp.dot(xn, prefix, precision=HIGHEST, preferred_element_type=F32)[0:1, :]
    eye = (lax.broadcasted_iota(jnp.int32, (n_rows, LANES), 0)
           == lax.broadcasted_iota(jnp.int32, (n_rows, LANES), 1))
    fq = jnp.sum(jnp.where(eye, fn_row, 0.0), axis=1, keepdims=True)

    assert n_pages * lc_refs[0].shape[1] == LANES
    x = jnp.concatenate([lc_refs[pg][0] for pg in range(n_pages)], axis=0)
    within = jnp.dot(x, (same_head & (li > lj)).astype(F32), precision=HIGHEST,
                     preferred_element_type=F32)
    tot = jnp.dot(x, same_head.astype(F32), precision=HIGHEST, preferred_element_type=F32)
    later_rows = jnp.dot((lj > li).astype(F32), tot, precision=HIGHEST, preferred_element_type=F32)
    g2 = (within + later_rows) * (-LOG2E)
    rows_per_page = lc_refs[0].shape[1]
    g_rows = [jnp.concatenate([g2[pg * rows_per_page + r:pg * rows_per_page + r + 1, :]
                               for r in range(rows_per_page)], axis=1)
              for pg in range(n_pages)]

    c_qk = scale * LOG2E
    fq2 = fq * LOG2E
    qb = stack_heads(q_ref).astype(BF16)
    ri = lax.broadcasted_iota(jnp.int32, (n_rows, page_cols), 0)
    ci = lax.broadcasted_iota(jnp.int32, (n_rows, page_cols), 1)
    head_ok = (ri >> tshift) == (ci & hmask)
    mx = jnp.full((n_rows, 1), -jnp.inf, F32)
    for pg in range(n_pages):
        s = lax.dot_general(qb, kc_refs[pg][0].astype(BF16), NT_DIMS, preferred_element_type=F32)
        s = jnp.where(head_ok, s * c_qk - g_rows[pg], -jnp.inf)
        s_scr[pg] = s
        mx = jnp.maximum(mx, jnp.max(s, axis=1, keepdims=True))

    zpad = jnp.zeros((LANES - n_rows, FOX_DH), BF16)
    kn = jnp.concatenate([stack_heads(kn_ref).astype(BF16), zpad], axis=0)
    vn = jnp.concatenate([stack_heads(vn_ref).astype(BF16), zpad], axis=0)
    r2 = lax.broadcasted_iota(jnp.int32, (n_rows, LANES), 0)
    c2 = lax.broadcasted_iota(jnp.int32, (n_rows, LANES), 1)
    ok = ((r2 >> tshift) == (c2 >> tshift)) & (c2 <= r2)
    s2 = lax.dot_general(qb, kn, NT_DIMS, preferred_element_type=F32)
    s2 = jnp.where(ok, s2 * c_qk - fn_row * LOG2E, -jnp.inf)
    mx = jnp.maximum(mx, jnp.max(s2, axis=1, keepdims=True))

    shift = fq2 - (mx + fq2)
    l = jnp.zeros((n_rows, 1), F32)
    acc = jnp.zeros((n_rows, FOX_DH), F32)
    for pg in range(n_pages):
        pr = jnp.exp2(s_scr[pg] + shift)
        l = l + jnp.sum(pr, axis=1, keepdims=True)
        acc = acc + jnp.dot(pr.astype(BF16), vc_refs[pg][0].astype(BF16), preferred_element_type=F32)
    p2 = jnp.exp2(s2 + shift)
    l = l + jnp.sum(p2, axis=1, keepdims=True)
    acc = acc + jnp.dot(p2.astype(BF16), vn, preferred_element_type=F32)
    out = acc / l
    for h in range(FOX_HEADS):
        o_ref[:, h * FOX_DH:(h + 1) * FOX_DH] = out[h * t_new:(h + 1) * t_new, :]


def _page_index(b, pt, *, j):
    return (pt[b, j], 0, 0)


def _fox_sample(page_table, fq, fk, fv, lf_new, cache_k, cache_v, cache_lf, *, t_new):
    dec_batch, n_pages = page_table.shape
    n_phys, page_size = cache_k.shape[0], cache_k.shape[1]
    n_rows = t_new * FOX_HEADS
    page_cols = page_size * FOX_HEADS
    ln = lf_new.reshape(dec_batch, t_new, FOX_HEADS).transpose(0, 2, 1).reshape(dec_batch, 1, n_rows)
    ln = jnp.pad(ln, ((0, 0), (0, 0), (0, LANES - n_rows)))
    kc = cache_k.reshape(n_phys, page_cols, FOX_DH)
    vc = cache_v.reshape(n_phys, page_cols, FOX_DH)
    lc = cache_lf.reshape(n_phys, page_cols // LANES, LANES)
    new_spec = pl.BlockSpec((t_new, FOX_W), lambda b, pt: (b, 0))
    kv_specs = [pl.BlockSpec((1, page_cols, FOX_DH), functools.partial(_page_index, j=j))
                for j in range(n_pages)]
    lf_specs = [pl.BlockSpec((1, page_cols // LANES, LANES), functools.partial(_page_index, j=j))
                for j in range(n_pages)]
    grid_spec = pltpu.PrefetchScalarGridSpec(
        num_scalar_prefetch=1,
        grid=(dec_batch,),
        in_specs=[new_spec, new_spec, new_spec, pl.BlockSpec((1, 1, LANES), lambda b, pt: (b, 0, 0))]
        + kv_specs + kv_specs + lf_specs,
        out_specs=new_spec,
        scratch_shapes=[pltpu.VMEM((n_pages, n_rows, page_cols), F32)],
    )
    return pl.pallas_call(
        functools.partial(_fox_sample_kernel, n_pages=n_pages, scale=FOX_DH ** -0.5,
                          page_cols=page_cols),
        grid_spec=grid_spec,
        out_shape=jax.ShapeDtypeStruct((dec_batch * t_new, FOX_W), F32),
        compiler_params=pltpu.CompilerParams(
            dimension_semantics=("parallel",), vmem_limit_bytes=56 * 1024 * 1024),
        name="fox_sample",
    )(page_table, fq, fk, fv, ln, *([kc] * n_pages), *([vc] * n_pages), *([lc] * n_pages))


def _merge_kernel(ro_ref, fo_ref, ga_ref, gb_ref, x_ref, wr_ref, wf_ref, wo_ref, g_ref, b_ref, h_ref):
    a = jnp.dot(ro_ref[...].astype(BF16), wr_ref[...], preferred_element_type=F32)
    b = jnp.dot(fo_ref[...].astype(BF16), wf_ref[...], preferred_element_type=F32)
    merged = jax.nn.sigmoid(ga_ref[...]) * a + jax.nn.sigmoid(gb_ref[...]) * b
    mix = jnp.dot(merged.astype(BF16), wo_ref[...], preferred_element_type=F32)
    h_ref[...] = _layer_norm(ALPHA * x_ref[...] + mix, g_ref[...], b_ref[...])


def _merge(ret_o, fox_o, ga, gb, x, wr, wf, wo, g, b, *, tm):
    m = x.shape[0]
    row = lambda w: pl.BlockSpec((tm, w), lambda i: (i, 0))
    return pl.pallas_call(
        _merge_kernel,
        grid=(m // tm,),
        in_specs=[row(RET_V_W), row(FOX_W), row(D_MODEL), row(D_MODEL), row(D_MODEL),
                  _resident(wr.shape), _resident(wf.shape), _resident(wo.shape),
                  _resident(g.shape), _resident(b.shape)],
        out_specs=row(D_MODEL),
        out_shape=jax.ShapeDtypeStruct((m, D_MODEL), F32),
        compiler_params=pltpu.CompilerParams(
            dimension_semantics=("parallel",), vmem_limit_bytes=48 * 1024 * 1024),
        name="merge_out_ln",
    )(ret_o, fox_o, ga, gb, x, wr, wf, wo, g, b)


def _ffn_kernel(h_ref, wu_ref, wd_ref, g_ref, b_ref, y_ref, *, ff_tile):
    h = h_ref[...]
    hb = h.astype(BF16)
    acc = jnp.zeros(h.shape, F32)
    for c in range(D_FF // ff_tile):
        sl = slice(c * ff_tile, (c + 1) * ff_tile)
        u = jnp.maximum(jnp.dot(hb, wu_ref[:, sl], preferred_element_type=F32), 0.0)
        acc = acc + jnp.dot((u * u).astype(BF16), wd_ref[sl, :], preferred_element_type=F32)
    y_ref[...] = _layer_norm(ALPHA * h + acc, g_ref[...], b_ref[...])


def _ffn(h, wu, wd, g, b, *, tm, ff_tile):
    m = h.shape[0]
    row = pl.BlockSpec((tm, D_MODEL), lambda i: (i, 0))
    return pl.pallas_call(
        functools.partial(_ffn_kernel, ff_tile=ff_tile),
        grid=(m // tm,),
        in_specs=[row, _resident(wu.shape), _resident(wd.shape), _resident(g.shape),
                  _resident(b.shape)],
        out_specs=row,
        out_shape=jax.ShapeDtypeStruct((m, D_MODEL), F32),
        compiler_params=pltpu.CompilerParams(
            dimension_semantics=("parallel",), vmem_limit_bytes=48 * 1024 * 1024),
        name="ffn_ln",
    )(h, wu, wd, g, b)


def _rope_tables(pos):
    half = RET_DK // 2
    inv_freq = ROPE_BASE ** (-jnp.arange(half, dtype=F32) / half)
    ang = pos.astype(F32)[:, None] * inv_freq[None, :]
    c, s = jnp.cos(ang), jnp.sin(ang)
    return jnp.concatenate([c, c], axis=1), jnp.concatenate([-s, s], axis=1)


def kernel(x_prompt, x_sample, cache_k, cache_v, cache_logf, state_ret, page_table, w_in, b_forget,
           ret_gn_gain, w_ret_proj, w_fox_proj, w_out, ln1_g, ln1_b, w_ff_up, w_ff_down, ln2_g,
           ln2_b):
    batch, seq, _ = x_prompt.shape
    dec_batch, t_new, _ = x_sample.shape
    page_size = cache_k.shape[2]
    past_len = page_table.shape[1] * page_size
    tm = 256

    w = w_in[0]
    ff_lo, ff_hi = 2 * RET_QK_W + 2 * RET_V_W + 3 * FOX_W, 2 * RET_QK_W + 2 * RET_V_W + 3 * FOX_W + FOX_HEADS
    w_pack = jnp.concatenate(
        [w[:, :ff_lo], w[:, ff_hi:], jnp.pad(w[:, ff_lo:ff_hi], ((0, 0), (0, LANES - FOX_HEADS)))],
        axis=1).astype(BF16)
    bf_pad = jnp.pad(b_forget[0].astype(F32), (0, LANES - FOX_HEADS)).reshape(1, LANES)
    gain = ret_gn_gain[0].reshape(1, RET_V_W)
    wr, wf, wo = w_ret_proj[0].astype(BF16), w_fox_proj[0].astype(BF16), w_out[0].astype(BF16)
    wu, wd = w_ff_up[0].astype(BF16), w_ff_down[0].astype(BF16)
    g1, b1 = ln1_g[0].reshape(1, D_MODEL), ln1_b[0].reshape(1, D_MODEL)
    g2, b2 = ln2_g[0].reshape(1, D_MODEL), ln2_b[0].reshape(1, D_MODEL)

    def tail(ret_o, fox_o, ga, gb, x2):
        h = _merge(ret_o, fox_o, ga, gb, x2, wr, wf, wo, g1, b1, tm=2 * tm)
        return _ffn(h, wu, wd, g2, b2, tm=2 * tm, ff_tile=4096)

    xp = x_prompt.reshape(batch * seq, D_MODEL)
    cos_p, sin_p = _rope_tables(jnp.arange(seq, dtype=jnp.int32))
    (rq, rk, rv, rg, fq, fk, fv, ga, gb, lf8, lfp, fkb, fvt) = _inproj(
        xp, w_pack, cos_p, sin_p, bf_pad, tm=tm, act_dtype=BF16, prompt_seq=seq)
    fcum, frow = _cumsum(lfp, batch=batch, seq=seq)
    ret_o, s_prompt = _ret_prompt(rq, rk, rv, rg, gain, batch=batch, seq=seq, chunks_per_step=4)
    fox_o = _fox_prompt(fq, fkb, fvt, fcum, frow, batch=batch, seq=seq, tile=512, heads_per_step=4)
    y_prompt = tail(ret_o, fox_o, ga, gb, xp).reshape(batch, seq, D_MODEL)

    xs = x_sample.reshape(dec_batch * t_new, D_MODEL)
    pos_s = past_len + (jnp.arange(dec_batch * t_new, dtype=jnp.int32) % t_new)
    cos_s, sin_s = _rope_tables(pos_s)
    (rq_s, rk_s, rv_s, rg_s, fq_s, fk_s, fv_s, ga_s, gb_s, lf8_s, _) = _inproj(
        xs, w_pack, cos_s, sin_s, bf_pad, tm=tm, act_dtype=F32)
    ret_o_s, s_sample = _ret_sample(rq_s, rk_s, rv_s, rg_s, gain, state_ret[0], t_new=t_new, group=16)
    fox_o_s = _fox_sample(page_table, fq_s, fk_s, fv_s, lf8_s, cache_k[0], cache_v[0], cache_logf[0],
                          t_new=t_new)
    y_sample = tail(ret_o_s, fox_o_s, ga_s, gb_s, xs).reshape(dec_batch, t_new, D_MODEL)

    return (y_prompt, y_sample,
            fk.reshape(1, batch, seq, FOX_HEADS, FOX_DH), fv.reshape(1, batch, seq, FOX_HEADS, FOX_DH),
            lf8.reshape(1, batch, seq, FOX_HEADS), s_prompt[None],
            fk_s.reshape(1, dec_batch, t_new, FOX_HEADS, FOX_DH),
            fv_s.reshape(1, dec_batch, t_new, FOX_HEADS, FOX_DH),
            lf8_s.reshape(1, dec_batch, t_new, FOX_HEADS), s_sample[None])
```

```python
import functools
import math

import jax
import jax.numpy as jnp
import numpy as np
from jax import lax
from jax.experimental import pallas as pl
from jax.experimental.pallas import tpu as pltpu

F32 = jnp.float32
BF16 = jnp.bfloat16
HIGHEST = lax.Precision.HIGHEST

D_MODEL = 1024
RET_HEADS = 4
RET_DK = 128
RET_DV = 256
FOX_HEADS = 8
FOX_DH = 128
D_FF = 4 * D_MODEL
CHUNK = 128
ROPE_BASE = 10000.0
LN_EPS = 1e-5
GN_EPS = 1e-6
DEPTH = 1
ALPHA = (2.0 * DEPTH) ** 0.25
RET_QK_W = RET_HEADS * RET_DK
RET_V_W = RET_HEADS * RET_DV
FOX_W = FOX_HEADS * FOX_DH
LOG2E = math.log2(math.e)
LANES = 128
SUBLANES = 8
VMEM_V7X_BYTES = 64 * 1024 * 1024
VMEM_LIMIT_LARGE = VMEM_V7X_BYTES * 7 // 8
VMEM_LIMIT_MEDIUM = VMEM_V7X_BYTES * 3 // 4

ROW_TILE = 256
TAIL_ROW_TILE = 512
ATTN_QUERY_TILE = 512
ATTN_HEADS_PER_STEP = 4
RET_CHUNKS_PER_STEP = 8
RET_SAMPLE_GROUP = 16

NT_DIMS = (((1,), (1,)), ((), ()))
TN_DIMS = (((0,), (0,)), ((), ()))

C_RQ = 0
C_RK = C_RQ + RET_QK_W
C_RV = C_RK + RET_QK_W
C_RG = C_RV + RET_V_W
C_FQ = C_RG + RET_V_W
C_FK = C_FQ + FOX_W
C_FV = C_FK + FOX_W
C_FF = C_FV + FOX_W
C_GATES = C_FF + FOX_HEADS


def _resident(shape):
    return pl.BlockSpec(shape, lambda *_: (0,) * len(shape), pipeline_mode=pl.Buffered(1))


def _log_sigmoid(z):
    return jnp.minimum(z, 0.0) - jnp.log1p(jnp.exp(-jnp.abs(z)))


def _layer_norm(z, g, b):
    mu = jnp.mean(z, axis=-1, keepdims=True)
    zc = z - mu
    var = jnp.mean(zc * zc, axis=-1, keepdims=True)
    return zc * lax.rsqrt(var + LN_EPS) * g + b


def _ret_log_decay(h):
    return math.log(1.0 - 2.0 ** (-5.0 - h))


def _inproj_kernel(x_ref, w_ref, wg_ref, wf_ref, cos_ref, sin_ref, bf_ref,
                   rq_ref, rk_ref, rv_ref, rg_ref, fq_ref, fk_ref, fv_ref,
                   ga_ref, gb_ref, lf8_ref, lfp_ref, *extra_refs):
    xb = x_ref[...].astype(BF16)

    def mm(lo, hi, ref=w_ref):
        return jnp.dot(xb, ref[:, lo:hi], preferred_element_type=F32)

    cos = cos_ref[...]
    sin = sin_ref[...]

    def rope(v):
        outs = []
        for h in range(RET_HEADS):
            vh = v[:, h * RET_DK:(h + 1) * RET_DK]
            outs.append(vh * cos + pltpu.roll(vh, RET_DK // 2, 1) * sin)
        return jnp.concatenate(outs, axis=1)

    rq_ref[...] = rope(mm(C_RQ, C_RK)).astype(rq_ref.dtype)
    rk_ref[...] = rope(mm(C_RK, C_RV)) * (RET_DK ** -0.5)
    rv_ref[...] = mm(C_RV, C_RG).astype(rv_ref.dtype)
    rg_ref[...] = mm(C_RG, C_FQ)
    fq_ref[...] = mm(C_FQ, C_FK).astype(fq_ref.dtype)
    fk = mm(C_FK, C_FV)
    fk_ref[...] = fk
    fv = mm(C_FV, C_FF)
    fv_ref[...] = fv
    if extra_refs:
        fkb_ref, fvt_ref = extra_refs
        fkb_ref[...] = fk.astype(BF16)
        for h in range(FOX_HEADS):
            fvt_ref[0, h, 0] = fv[:, h * FOX_DH:(h + 1) * FOX_DH].T.astype(BF16)
    ga_ref[...] = mm(0, D_MODEL, wg_ref)
    gb_ref[...] = mm(D_MODEL, 2 * D_MODEL, wg_ref)
    lf = _log_sigmoid(mm(0, LANES, wf_ref) + bf_ref[...])
    lfp_ref[...] = lf
    lf8_ref[...] = lf[:, :FOX_HEADS]


def _inproj(x, weights, cos_t, sin_t, bf_pad, *, tm, act_dtype, prompt_seq=None):
    w_main, w_gates, w_ff = weights
    m = x.shape[0]
    n_tab = cos_t.shape[0] // tm
    row = lambda w: pl.BlockSpec((tm, w), lambda i: (i, 0))
    tab = pl.BlockSpec((tm, LANES), lambda i: (i % n_tab, 0))
    out_shape = [
        jax.ShapeDtypeStruct((m, RET_QK_W), act_dtype),
        jax.ShapeDtypeStruct((m, RET_QK_W), F32),
        jax.ShapeDtypeStruct((m, RET_V_W), act_dtype),
        jax.ShapeDtypeStruct((m, RET_V_W), F32),
        jax.ShapeDtypeStruct((m, FOX_W), act_dtype),
        jax.ShapeDtypeStruct((m, FOX_W), F32),
        jax.ShapeDtypeStruct((m, FOX_W), F32),
        jax.ShapeDtypeStruct((m, D_MODEL), F32),
        jax.ShapeDtypeStruct((m, D_MODEL), F32),
        jax.ShapeDtypeStruct((m, FOX_HEADS), F32),
        jax.ShapeDtypeStruct((m, LANES), F32),
    ]
    out_specs = [row(RET_QK_W), row(RET_QK_W), row(RET_V_W), row(RET_V_W), row(FOX_W),
                 row(FOX_W), row(FOX_W), row(D_MODEL), row(D_MODEL), row(FOX_HEADS), row(LANES)]
    if prompt_seq is not None:
        nt = prompt_seq // tm
        out_shape += [
            jax.ShapeDtypeStruct((m, FOX_W), BF16),
            jax.ShapeDtypeStruct((m // prompt_seq, FOX_HEADS, nt, FOX_DH, tm), BF16),
        ]
        out_specs += [row(FOX_W),
                      pl.BlockSpec((1, FOX_HEADS, 1, FOX_DH, tm), lambda i: (i // nt, 0, i % nt, 0, 0))]
    return pl.pallas_call(
        _inproj_kernel,
        grid=(m // tm,),
        in_specs=[row(D_MODEL), _resident(w_main.shape), _resident(w_gates.shape),
                  _resident(w_ff.shape), tab, tab, _resident(bf_pad.shape)],
        out_specs=out_specs,
        out_shape=out_shape,
        compiler_params=pltpu.CompilerParams(
            dimension_semantics=("parallel",), vmem_limit_bytes=VMEM_LIMIT_LARGE),
        name="inproj",
    )(x, w_main, w_gates, w_ff, cos_t, sin_t, bf_pad)


def _cumsum_kernel(lf_ref, fcol_ref, frow_ref, *, blk):
    t = lf_ref.shape[0]
    ri = lax.broadcasted_iota(jnp.int32, (blk, blk), 0)
    ci = lax.broadcasted_iota(jnp.int32, (blk, blk), 1)
    tri = (ri >= ci).astype(F32)
    carry = jnp.zeros((1, LANES), F32)
    for b in range(t // blk):
        xb = lf_ref[b * blk:(b + 1) * blk, :]
        fb = jnp.dot(tri, xb, precision=HIGHEST, preferred_element_type=F32) + carry
        fcol_ref[b * blk:(b + 1) * blk, :] = fb
        frow_ref[:, b * blk:(b + 1) * blk] = fb.T[:FOX_HEADS, :]
        carry = fb[blk - 1:blk, :]


def _cumsum(lf_pad, *, batch, seq):
    return pl.pallas_call(
        functools.partial(_cumsum_kernel, blk=LANES),
        grid=(batch,),
        in_specs=[pl.BlockSpec((seq, LANES), lambda b: (b, 0))],
        out_specs=(pl.BlockSpec((seq, LANES), lambda b: (b, 0)),
                   pl.BlockSpec((FOX_HEADS, seq), lambda b: (b, 0))),
        out_shape=(jax.ShapeDtypeStruct((batch * seq, LANES), F32),
                   jax.ShapeDtypeStruct((batch * FOX_HEADS, seq), F32)),
        compiler_params=pltpu.CompilerParams(dimension_semantics=("parallel",)),
        name="logf_cumsum",
    )(lf_pad)


def _ret_head_out(o, gain, rg):
    mu = jnp.mean(o, axis=-1, keepdims=True)
    oc = o - mu
    var = jnp.mean(oc * oc, axis=-1, keepdims=True)
    y = oc * lax.rsqrt(var + GN_EPS) * gain
    return y * (rg * jax.nn.sigmoid(rg))


def _ret_prompt_kernel(q_ref, k_ref, v_ref, rg_ref, gain_ref, o_ref, sfin_ref, s_scr, *, chunk):
    c = pl.program_id(1)

    @pl.when(c == 0)
    def _():
        s_scr[...] = jnp.zeros_like(s_scr)

    ii = lax.broadcasted_iota(jnp.int32, (chunk, chunk), 0)
    jj = lax.broadcasted_iota(jnp.int32, (chunk, chunk), 1)
    causal = ii >= jj
    expo = jnp.where(causal, (ii - jj).astype(F32), 0.0)
    ic = lax.broadcasted_iota(jnp.int32, (chunk, 1), 0).astype(F32)
    for h in range(RET_HEADS):
        ld = _ret_log_decay(h)
        dmat = jnp.where(causal, jnp.exp(expo * ld), 0.0)
        cross_decay = jnp.exp((ic + 1.0) * ld)
        k_decay = jnp.exp((chunk - 1.0 - ic) * ld)
        chunk_decay = math.exp(chunk * ld)
        s_run = s_scr[h]
        for cc in range(q_ref.shape[0] // chunk):
            rows = slice(cc * chunk, (cc + 1) * chunk)
            qh = q_ref[rows, h * RET_DK:(h + 1) * RET_DK].astype(BF16)
            kh = k_ref[rows, h * RET_DK:(h + 1) * RET_DK]
            vh = v_ref[rows, h * RET_DV:(h + 1) * RET_DV].astype(BF16)
            sc = lax.dot_general(qh, kh.astype(BF16), NT_DIMS, preferred_element_type=F32) * dmat
            intra = jnp.dot(sc.astype(BF16), vh, preferred_element_type=F32)
            cross = jnp.dot(qh, s_run.astype(BF16), preferred_element_type=F32) * cross_decay
            kd = (kh * k_decay).astype(BF16)
            s_run = s_run * chunk_decay + lax.dot_general(kd, vh, TN_DIMS, preferred_element_type=F32)
            sl = slice(h * RET_DV, (h + 1) * RET_DV)
            o_ref[rows, sl] = _ret_head_out(
                intra + cross, gain_ref[:, sl], rg_ref[rows, sl]).astype(o_ref.dtype)
        s_scr[h] = s_run

    @pl.when(c == pl.num_programs(1) - 1)
    def _():
        sfin_ref[0] = s_scr[...]


def _ret_prompt(rq, rk, rv, rg, gain, *, batch, seq, chunks_per_step):
    rows = CHUNK * chunks_per_step
    nc = seq // rows
    row = lambda w: pl.BlockSpec((rows, w), lambda b, c: (b * nc + c, 0))
    return pl.pallas_call(
        functools.partial(_ret_prompt_kernel, chunk=CHUNK),
        grid=(batch, nc),
        in_specs=[row(RET_QK_W), row(RET_QK_W), row(RET_V_W), row(RET_V_W),
                  pl.BlockSpec((1, RET_V_W), lambda b, c: (0, 0))],
        out_specs=(row(RET_V_W),
                   pl.BlockSpec((1, RET_HEADS, RET_DK, RET_DV), lambda b, c: (b, 0, 0, 0))),
        out_shape=(jax.ShapeDtypeStruct((batch * seq, RET_V_W), BF16),
                   jax.ShapeDtypeStruct((batch, RET_HEADS, RET_DK, RET_DV), F32)),
        scratch_shapes=[pltpu.VMEM((RET_HEADS, RET_DK, RET_DV), F32)],
        compiler_params=pltpu.CompilerParams(
            dimension_semantics=("parallel", "arbitrary"), vmem_limit_bytes=VMEM_LIMIT_MEDIUM),
        name="ret_prompt",
    )(rq, rk, rv, rg, gain)


def _ret_sample_kernel(q_ref, k_ref, v_ref, rg_ref, gain_ref, s0_ref, o_ref, s1_ref, *, group, t_new):
    rows = group * t_new
    shift = t_new.bit_length() - 1
    ri = lax.broadcasted_iota(jnp.int32, (rows, rows), 0)
    ci = lax.broadcasted_iota(jnp.int32, (rows, rows), 1)
    valid = ((ri >> shift) == (ci >> shift)) & (ri >= ci)
    expo = jnp.where(valid, (ri - ci).astype(F32), 0.0)
    r1 = lax.broadcasted_iota(jnp.int32, (rows, 1), 0)
    tpos = (r1 & (t_new - 1)).astype(F32)
    rowgrp = lax.broadcasted_iota(jnp.int32, (rows, RET_DK), 0) >> shift
    for h in range(RET_HEADS):
        ld = _ret_log_decay(h)
        dmat = jnp.where(valid, jnp.exp(expo * ld), 0.0)
        cross_decay = jnp.exp((tpos + 1.0) * ld)
        k_decay = jnp.exp((t_new - 1.0 - tpos) * ld)
        chunk_decay = math.exp(t_new * ld)
        qh = q_ref[:, h * RET_DK:(h + 1) * RET_DK]
        kh = k_ref[:, h * RET_DK:(h + 1) * RET_DK]
        vh = v_ref[:, h * RET_DV:(h + 1) * RET_DV].astype(BF16)
        qb = qh.astype(BF16)
        sc = lax.dot_general(qb, kh.astype(BF16), NT_DIMS, preferred_element_type=F32) * dmat
        o = jnp.dot(sc.astype(BF16), vh, preferred_element_type=F32)
        kd = kh * k_decay
        cross = jnp.zeros((rows, RET_DV), F32)
        for g in range(group):
            s_old = s0_ref[g, h]
            in_g = rowgrp == g
            qg = jnp.where(in_g, qh, 0.0).astype(BF16)
            cross = cross + jnp.dot(qg, s_old.astype(BF16), preferred_element_type=F32)
            kg = jnp.where(in_g, kd, 0.0).astype(BF16)
            s1_ref[g, h] = s_old * chunk_decay + lax.dot_general(
                kg, vh, TN_DIMS, preferred_element_type=F32)
        o = o + cross * cross_decay
        sl = slice(h * RET_DV, (h + 1) * RET_DV)
        o_ref[:, sl] = _ret_head_out(o, gain_ref[:, sl], rg_ref[:, sl]).astype(o_ref.dtype)


def _ret_sample(rq, rk, rv, rg, gain, state, *, t_new, group):
    m = rq.shape[0]
    rows = group * t_new
    row = lambda w: pl.BlockSpec((rows, w), lambda i: (i, 0))
    st = pl.BlockSpec((group, RET_HEADS, RET_DK, RET_DV), lambda i: (i, 0, 0, 0))
    return pl.pallas_call(
        functools.partial(_ret_sample_kernel, group=group, t_new=t_new),
        grid=(m // rows,),
        in_specs=[row(RET_QK_W), row(RET_QK_W), row(RET_V_W), row(RET_V_W),
                  pl.BlockSpec((1, RET_V_W), lambda i: (0, 0)), st],
        out_specs=(row(RET_V_W), st),
        out_shape=(jax.ShapeDtypeStruct((m, RET_V_W), BF16),
                   jax.ShapeDtypeStruct(state.shape, F32)),
        compiler_params=pltpu.CompilerParams(
            dimension_semantics=("parallel",), vmem_limit_bytes=VMEM_LIMIT_MEDIUM),
        name="ret_sample",
    )(rq, rk, rv, rg, gain, state)


def _fox_prompt_kernel(q_ref, k_ref, vt_ref, fcum_ref, frow_ref, o_ref,
                       fk_scr, t_scr, p_scr, *, tile, scale):
    hps = vt_ref.shape[1]
    h0 = pl.program_id(1) * hps
    i = pl.program_id(2)
    seq = k_ref.shape[0]
    vt_tile = vt_ref.shape[-1]
    per = tile // vt_tile
    assert per % 2 == 0

    @pl.when(i == 0)
    def _():
        lane = lax.broadcasted_iota(jnp.int32, (seq, LANES), 1)
        for g in range(hps):
            col = jnp.sum(jnp.where(lane == h0 + g, fcum_ref[...], 0.0), axis=1, keepdims=True)
            fk_scr[g] = jnp.broadcast_to(col * LOG2E, (seq, LANES))

    def scores(g, u):
        start = pl.multiple_of(u * vt_tile, vt_tile)
        ks = k_ref[pl.ds(start, vt_tile), g * FOX_DH:(g + 1) * FOX_DH]
        fk2 = fk_scr[g, pl.ds(start, vt_tile), :]
        t = lax.dot_general(ks, q_ref[:, g * FOX_DH:(g + 1) * FOX_DH], NT_DIMS,
                            preferred_element_type=F32) * (scale * LOG2E)
        return t - jnp.concatenate([fk2] * (tile // LANES), axis=1)

    def sub_step(g, u, slot, state, *, diag_part=None, last=False):
        m_run, l_run, acc, alpha_prev = state
        fq2 = frow_ref[g, pl.ds(i, 1), :] * LOG2E
        if not last:
            t_scr[g, 1 - slot] = scores(g, u + 1)
        pv_prev = jnp.dot(vt_ref[0, g, jnp.maximum(u - 1, 0)], p_scr[g, 1 - slot],
                          preferred_element_type=F32)
        acc = alpha_prev * acc + pv_prev
        t = t_scr[g, slot]
        if diag_part is not None:
            key = diag_part * vt_tile + lax.broadcasted_iota(jnp.int32, (vt_tile, tile), 0)
            qry = lax.broadcasted_iota(jnp.int32, (vt_tile, tile), 1)
            t = jnp.where(key <= qry, t, -jnp.inf)
        m_new = jnp.maximum(m_run, jnp.max(t, axis=0, keepdims=True) + fq2)
        alpha = jnp.exp2(m_run - m_new)
        p = jnp.exp2(t + (fq2 - m_new))
        l_new = alpha * l_run + jnp.sum(p, axis=0, keepdims=True)
        p_scr[g, slot] = p.astype(BF16)
        return m_new, l_new, acc, alpha

    def pair(j, states):
        for s in range(per):
            states = tuple(sub_step(g, j * per + s, s % 2, states[g]) for g in range(hps))
        return states

    for g in range(hps):
        t_scr[g, 0] = scores(g, 0)
        p_scr[g, 1] = jnp.zeros((vt_tile, tile), BF16)
    init = (jnp.full((1, tile), -jnp.inf, F32), jnp.zeros((1, tile), F32),
            jnp.zeros((FOX_DH, tile), F32), jnp.ones((1, tile), F32))
    states = lax.fori_loop(0, i, pair, (init,) * hps)
    for s in range(per):
        states = tuple(sub_step(g, i * per + s, s % 2, states[g], diag_part=s, last=(s == per - 1))
                       for g in range(hps))
    last_u = i * per + per - 1
    for g in range(hps):
        _, l_fin, acc, alpha_prev = states[g]
        acc = alpha_prev * acc + jnp.dot(vt_ref[0, g, last_u], p_scr[g, (per - 1) % 2],
                                         preferred_element_type=F32)
        o_ref[:, g * FOX_DH:(g + 1) * FOX_DH] = (acc / l_fin).T.astype(o_ref.dtype)


def _fox_prompt(fq, fkb, fvt, fcum, frow, *, batch, seq, tile, heads_per_step):
    nq = seq // tile
    hps = heads_per_step
    frow3 = frow.reshape(batch * FOX_HEADS, nq, tile)
    nt, vt_tile = fvt.shape[2], fvt.shape[4]
    return pl.pallas_call(
        functools.partial(_fox_prompt_kernel, tile=tile, scale=FOX_DH ** -0.5),
        grid=(batch, FOX_HEADS // hps, nq),
        in_specs=[pl.BlockSpec((tile, hps * FOX_DH), lambda b, h, i: (b * nq + i, h)),
                  pl.BlockSpec((seq, hps * FOX_DH), lambda b, h, i: (b, h)),
                  pl.BlockSpec((1, hps, nt, FOX_DH, vt_tile), lambda b, h, i: (b, h, 0, 0, 0)),
                  pl.BlockSpec((seq, LANES), lambda b, h, i: (b, 0)),
                  pl.BlockSpec((hps, nq, tile), lambda b, h, i: (b * (FOX_HEADS // hps) + h, 0, 0))],
        out_specs=pl.BlockSpec((tile, hps * FOX_DH), lambda b, h, i: (b * nq + i, h)),
        out_shape=jax.ShapeDtypeStruct((batch * seq, FOX_W), BF16),
        scratch_shapes=[pltpu.VMEM((hps, seq, LANES), F32),
                        pltpu.VMEM((hps, 2, vt_tile, tile), F32),
                        pltpu.VMEM((hps, 2, vt_tile, tile), BF16)],
        compiler_params=pltpu.CompilerParams(
            dimension_semantics=("parallel", "parallel", "arbitrary")),
        name="fox_prompt",
    )(fq, fkb, fvt, fcum, frow3)


def _fox_sample_kernel(pt_ref, q_ref, kn_ref, vn_ref, ln_ref, *refs, n_pages, scale, page_cols):
    del pt_ref
    kc_refs = refs[:n_pages]
    vc_refs = refs[n_pages:2 * n_pages]
    lc_refs = refs[2 * n_pages:3 * n_pages]
    o_ref, s_scr = refs[3 * n_pages:]
    t_new = q_ref.shape[0]
    assert t_new == SUBLANES
    n_rows = FOX_HEADS * t_new
    tshift = t_new.bit_length() - 1
    hmask = FOX_HEADS - 1
    li = lax.broadcasted_iota(jnp.int32, (LANES, LANES), 0)
    lj = lax.broadcasted_iota(jnp.int32, (LANES, LANES), 1)
    same_head = ((li - lj) & hmask) == 0

    def stack_heads(ref):
        return jnp.concatenate([ref[:, h * FOX_DH:(h + 1) * FOX_DH] for h in range(FOX_HEADS)], axis=0)

    prefix = (((li >> tshift) == (lj >> tshift)) & (li <= lj)).astype(F32)
    xn = jnp.broadcast_to(ln_ref[0], (SUBLANES, LANES))
    fn_row = jnp.dot(xn, prefix, precision=HIGHEST, preferred_element_type=F32)[0:1, :]
    eye = (lax.broadcasted_iota(jnp.int32, (n_rows, LANES), 0)
           == lax.broadcasted_iota(jnp.int32, (n_rows, LANES), 1))
    fq = jnp.sum(jnp.where(eye, fn_row, 0.0), axis=1, keepdims=True)

    assert n_pages * lc_refs[0].shape[1] == LANES
    x = jnp.concatenate([lc_refs[pg][0] for pg in range(n_pages)], axis=0)
    within = jnp.dot(x, (same_head & (li > lj)).astype(F32), precision=HIGHEST,
                     preferred_element_type=F32)
    tot = jnp.dot(x, same_head.astype(F32), precision=HIGHEST, preferred_element_type=F32)
    later_rows = jnp.dot((lj > li).astype(F32), tot, precision=HIGHEST, preferred_element_type=F32)
    g2 = (within + later_rows) * (-LOG2E)
    rows_per_page = lc_refs[0].shape[1]
    g_rows = [jnp.concatenate([g2[pg * rows_per_page + r:pg * rows_per_page + r + 1, :]
                               for r in range(rows_per_page)], axis=1)
              for pg in range(n_pages)]

    c_qk = scale * LOG2E
    fq2 = fq * LOG2E
    qb = stack_heads(q_ref).astype(BF16)
    ri = lax.broadcasted_iota(jnp.int32, (n_rows, page_cols), 0)
    ci = lax.broadcasted_iota(jnp.int32, (n_rows, page_cols), 1)
    head_ok = (ri >> tshift) == (ci & hmask)
    mx = jnp.full((n_rows, 1), -jnp.inf, F32)
    for pg in range(n_pages):
        s = lax.dot_general(qb, kc_refs[pg][0].astype(BF16), NT_DIMS, preferred_element_type=F32)
        s = jnp.where(head_ok, s * c_qk - g_rows[pg], -jnp.inf)
        s_scr[pg] = s
        mx = jnp.maximum(mx, jnp.max(s, axis=1, keepdims=True))

    zpad = jnp.zeros((LANES - n_rows, FOX_DH), BF16)
    kn = jnp.concatenate([stack_heads(kn_ref).astype(BF16), zpad], axis=0)
    vn = jnp.concatenate([stack_heads(vn_ref).astype(BF16), zpad], axis=0)
    r2 = lax.broadcasted_iota(jnp.int32, (n_rows, LANES), 0)
    c2 = lax.broadcasted_iota(jnp.int32, (n_rows, LANES), 1)
    ok = ((r2 >> tshift) == (c2 >> tshift)) & (c2 <= r2)
    s2 = lax.dot_general(qb, kn, NT_DIMS, preferred_element_type=F32)
    s2 = jnp.where(ok, s2 * c_qk - fn_row * LOG2E, -jnp.inf)
    mx = jnp.maximum(mx, jnp.max(s2, axis=1, keepdims=True))

    shift = fq2 - (mx + fq2)
    l = jnp.zeros((n_rows, 1), F32)
    acc = jnp.zeros((n_rows, FOX_DH), F32)
    for pg in range(n_pages):
        pr = jnp.exp2(s_scr[pg] + shift)
        l = l + jnp.sum(pr, axis=1, keepdims=True)
        acc = acc + jnp.dot(pr.astype(BF16), vc_refs[pg][0].astype(BF16), preferred_element_type=F32)
    p2 = jnp.exp2(s2 + shift)
    l = l + jnp.sum(p2, axis=1, keepdims=True)
    acc = acc + jnp.dot(p2.astype(BF16), vn, preferred_element_type=F32)
    out = acc / l
    for h in range(FOX_HEADS):
        o_ref[:, h * FOX_DH:(h + 1) * FOX_DH] = out[h * t_new:(h + 1) * t_new, :]


def _page_index(b, pt, *, j):
    return (pt[b, j], 0, 0)


def _fox_sample(page_table, fq, fk, fv, lf_new, cache_k, cache_v, cache_lf, *, t_new):
    dec_batch, n_pages = page_table.shape
    n_phys, page_size = cache_k.shape[0], cache_k.shape[1]
    n_rows = t_new * FOX_HEADS
    page_cols = page_size * FOX_HEADS
    ln = lf_new.reshape(dec_batch, t_new, FOX_HEADS).transpose(0, 2, 1).reshape(dec_batch, 1, n_rows)
    ln = jnp.pad(ln, ((0, 0), (0, 0), (0, LANES - n_rows)))
    kc = cache_k.reshape(n_phys, page_cols, FOX_DH)
    vc = cache_v.reshape(n_phys, page_cols, FOX_DH)
    lc = cache_lf.reshape(n_phys, page_cols // LANES, LANES)
    new_spec = pl.BlockSpec((t_new, FOX_W), lambda b, pt: (b, 0))
    kv_specs = [pl.BlockSpec((1, page_cols, FOX_DH), functools.partial(_page_index, j=j))
                for j in range(n_pages)]
    lf_specs = [pl.BlockSpec((1, page_cols // LANES, LANES), functools.partial(_page_index, j=j))
                for j in range(n_pages)]
    grid_spec = pltpu.PrefetchScalarGridSpec(
        num_scalar_prefetch=1,
        grid=(dec_batch,),
        in_specs=[new_spec, new_spec, new_spec, pl.BlockSpec((1, 1, LANES), lambda b, pt: (b, 0, 0))]
        + kv_specs + kv_specs + lf_specs,
        out_specs=new_spec,
        scratch_shapes=[pltpu.VMEM((n_pages, n_rows, page_cols), F32)],
    )
    return pl.pallas_call(
        functools.partial(_fox_sample_kernel, n_pages=n_pages, scale=FOX_DH ** -0.5,
                          page_cols=page_cols),
        grid_spec=grid_spec,
        out_shape=jax.ShapeDtypeStruct((dec_batch * t_new, FOX_W), F32),
        compiler_params=pltpu.CompilerParams(
            dimension_semantics=("parallel",), vmem_limit_bytes=VMEM_LIMIT_LARGE),
        name="fox_sample",
    )(page_table, fq, fk, fv, ln, *([kc] * n_pages), *([vc] * n_pages), *([lc] * n_pages))


def _merge_kernel(ro_ref, fo_ref, ga_ref, gb_ref, x_ref, wr_ref, wf_ref, wo_ref, g_ref, b_ref, h_ref):
    a = jnp.dot(ro_ref[...].astype(BF16), wr_ref[...], preferred_element_type=F32)
    b = jnp.dot(fo_ref[...].astype(BF16), wf_ref[...], preferred_element_type=F32)
    merged = jax.nn.sigmoid(ga_ref[...]) * a + jax.nn.sigmoid(gb_ref[...]) * b
    mix = jnp.dot(merged.astype(BF16), wo_ref[...], preferred_element_type=F32)
    h_ref[...] = _layer_norm(ALPHA * x_ref[...] + mix, g_ref[...], b_ref[...])


def _merge(ret_o, fox_o, ga, gb, x, wr, wf, wo, g, b, *, tm):
    m = x.shape[0]
    row = lambda w: pl.BlockSpec((tm, w), lambda i: (i, 0))
    return pl.pallas_call(
        _merge_kernel,
        grid=(m // tm,),
        in_specs=[row(RET_V_W), row(FOX_W), row(D_MODEL), row(D_MODEL), row(D_MODEL),
                  _resident(wr.shape), _resident(wf.shape), _resident(wo.shape),
                  _resident(g.shape), _resident(b.shape)],
        out_specs=row(D_MODEL),
        out_shape=jax.ShapeDtypeStruct((m, D_MODEL), F32),
        compiler_params=pltpu.CompilerParams(
            dimension_semantics=("parallel",), vmem_limit_bytes=VMEM_LIMIT_MEDIUM),
        name="merge_out_ln",
    )(ret_o, fox_o, ga, gb, x, wr, wf, wo, g, b)


def _ffn_kernel(h_ref, wu_ref, wd_ref, g_ref, b_ref, y_ref, *, ff_tile):
    h = h_ref[...]
    hb = h.astype(BF16)
    acc = jnp.zeros(h.shape, F32)
    for c in range(D_FF // ff_tile):
        sl = slice(c * ff_tile, (c + 1) * ff_tile)
        u = jnp.maximum(jnp.dot(hb, wu_ref[:, sl], preferred_element_type=F32), 0.0)
        acc = acc + jnp.dot((u * u).astype(BF16), wd_ref[sl, :], preferred_element_type=F32)
    y_ref[...] = _layer_norm(ALPHA * h + acc, g_ref[...], b_ref[...])


def _ffn(h, wu, wd, g, b, *, tm, ff_tile):
    m = h.shape[0]
    row = pl.BlockSpec((tm, D_MODEL), lambda i: (i, 0))
    return pl.pallas_call(
        functools.partial(_ffn_kernel, ff_tile=ff_tile),
        grid=(m // tm,),
        in_specs=[row, _resident(wu.shape), _resident(wd.shape), _resident(g.shape),
                  _resident(b.shape)],
        out_specs=row,
        out_shape=jax.ShapeDtypeStruct((m, D_MODEL), F32),
        compiler_params=pltpu.CompilerParams(
            dimension_semantics=("parallel",), vmem_limit_bytes=VMEM_LIMIT_MEDIUM),
        name="ffn_ln",
    )(h, wu, wd, g, b)


def _rope_tables(pos):
    half = RET_DK // 2
    inv_freq = ROPE_BASE ** (-np.arange(half, dtype=np.float64) / half)
    ang = np.asarray(pos, np.float64)[:, None] * inv_freq[None, :]
    c, s = np.cos(ang), np.sin(ang)
    return (jnp.asarray(np.concatenate([c, c], axis=1), F32),
            jnp.asarray(np.concatenate([-s, s], axis=1), F32))


def kernel(x_prompt, x_sample, cache_k, cache_v, cache_logf, state_ret, page_table, w_in, b_forget,
           ret_gn_gain, w_ret_proj, w_fox_proj, w_out, ln1_g, ln1_b, w_ff_up, w_ff_down, ln2_g,
           ln2_b):
    batch, seq, _ = x_prompt.shape
    dec_batch, t_new, _ = x_sample.shape
    page_size = cache_k.shape[2]
    past_len = page_table.shape[1] * page_size
    tm = ROW_TILE
    assert seq % ATTN_QUERY_TILE == 0 and seq % (CHUNK * RET_CHUNKS_PER_STEP) == 0
    assert (batch * seq) % TAIL_ROW_TILE == 0 and (dec_batch * t_new) % TAIL_ROW_TILE == 0
    assert dec_batch % RET_SAMPLE_GROUP == 0 and tm % t_new == 0

    w = w_in[0]
    w_pack = (w[:, :C_FF].astype(BF16), w[:, C_GATES:].astype(BF16),
              jnp.pad(w[:, C_FF:C_GATES], ((0, 0), (0, LANES - FOX_HEADS))).astype(BF16))
    bf_pad = jnp.pad(b_forget[0].astype(F32), (0, LANES - FOX_HEADS)).reshape(1, LANES)
    gain = ret_gn_gain[0].reshape(1, RET_V_W)
    wr, wf, wo = w_ret_proj[0].astype(BF16), w_fox_proj[0].astype(BF16), w_out[0].astype(BF16)
    wu, wd = w_ff_up[0].astype(BF16), w_ff_down[0].astype(BF16)
    g1, b1 = ln1_g[0].reshape(1, D_MODEL), ln1_b[0].reshape(1, D_MODEL)
    g2, b2 = ln2_g[0].reshape(1, D_MODEL), ln2_b[0].reshape(1, D_MODEL)

    def tail(ret_o, fox_o, ga, gb, x2):
        h = _merge(ret_o, fox_o, ga, gb, x2, wr, wf, wo, g1, b1, tm=TAIL_ROW_TILE)
        return _ffn(h, wu, wd, g2, b2, tm=TAIL_ROW_TILE, ff_tile=D_FF)

    xp = x_prompt.reshape(batch * seq, D_MODEL)
    cos_p, sin_p = _rope_tables(np.arange(seq))
    (rq, rk, rv, rg, fq, fk, fv, ga, gb, lf8, lfp, fkb, fvt) = _inproj(
        xp, w_pack, cos_p, sin_p, bf_pad, tm=tm, act_dtype=BF16, prompt_seq=seq)
    fcum, frow = _cumsum(lfp, batch=batch, seq=seq)
    ret_o, s_prompt = _ret_prompt(rq, rk, rv, rg, gain, batch=batch, seq=seq,
                                  chunks_per_step=RET_CHUNKS_PER_STEP)
    fox_o = _fox_prompt(fq, fkb, fvt, fcum, frow, batch=batch, seq=seq, tile=ATTN_QUERY_TILE,
                        heads_per_step=ATTN_HEADS_PER_STEP)
    y_prompt = tail(ret_o, fox_o, ga, gb, xp).reshape(batch, seq, D_MODEL)

    xs = x_sample.reshape(dec_batch * t_new, D_MODEL)
    cos_s, sin_s = _rope_tables(past_len + np.arange(tm) % t_new)
    (rq_s, rk_s, rv_s, rg_s, fq_s, fk_s, fv_s, ga_s, gb_s, lf8_s, _) = _inproj(
        xs, w_pack, cos_s, sin_s, bf_pad, tm=tm, act_dtype=F32)
    ret_o_s, s_sample = _ret_sample(rq_s, rk_s, rv_s, rg_s, gain, state_ret[0], t_new=t_new,
                                    group=RET_SAMPLE_GROUP)
    fox_o_s = _fox_sample(page_table, fq_s, fk_s, fv_s, lf8_s, cache_k[0], cache_v[0], cache_logf[0],
                          t_new=t_new)
    y_sample = tail(ret_o_s, fox_o_s, ga_s, gb_s, xs).reshape(dec_batch, t_new, D_MODEL)

    return (y_prompt, y_sample,
            fk.reshape(1, batch, seq, FOX_HEADS, FOX_DH), fv.reshape(1, batch, seq, FOX_HEADS, FOX_DH),
            lf8.reshape(1, batch, seq, FOX_HEADS), s_prompt[None],
            fk_s.reshape(1, dec_batch, t_new, FOX_HEADS, FOX_DH),
            fv_s.reshape(1, dec_batch, t_new, FOX_HEADS, FOX_DH),
            lf8_s.reshape(1, dec_batch, t_new, FOX_HEADS), s_sample[None])
```

```python
import functools
import math

import jax
import jax.numpy as jnp
import numpy as np
from jax import lax
from jax.experimental import pallas as pl
from jax.experimental.pallas import tpu as pltpu

F32 = jnp.float32
BF16 = jnp.bfloat16
HIGHEST = lax.Precision.HIGHEST

D_MODEL = 1024
RET_HEADS = 4
RET_DK = 128
RET_DV = 256
FOX_HEADS = 8
FOX_DH = 128
D_FF = 4 * D_MODEL
CHUNK = 128
ROPE_BASE = 10000.0
LN_EPS = 1e-5
GN_EPS = 1e-6
DEPTH = 1
ALPHA = (2.0 * DEPTH) ** 0.25
RET_QK_W = RET_HEADS * RET_DK
RET_V_W = RET_HEADS * RET_DV
FOX_W = FOX_HEADS * FOX_DH
LOG2E = math.log2(math.e)
LANES = 128
SUBLANES = 8
VMEM_V7X_BYTES = 64 * 1024 * 1024
VMEM_LIMIT_LARGE = VMEM_V7X_BYTES * 7 // 8
VMEM_LIMIT_MEDIUM = VMEM_V7X_BYTES * 3 // 4

ROW_TILE = 256
TAIL_ROW_TILE = 512
ATTN_QUERY_TILE = 512
ATTN_HEADS_PER_STEP = 4
RET_CHUNKS_PER_STEP = 8
RET_SAMPLE_GROUP = 16

NT_DIMS = (((1,), (1,)), ((), ()))
TN_DIMS = (((0,), (0,)), ((), ()))

C_RQ = 0
C_RK = C_RQ + RET_QK_W
C_RV = C_RK + RET_QK_W
C_RG = C_RV + RET_V_W
C_FQ = C_RG + RET_V_W
C_FK = C_FQ + FOX_W
C_FV = C_FK + FOX_W
C_FF = C_FV + FOX_W
C_GATES = C_FF + FOX_HEADS


def _resident(shape):
    return pl.BlockSpec(shape, lambda *_: (0,) * len(shape), pipeline_mode=pl.Buffered(1))


def _log_sigmoid(z):
    return jnp.minimum(z, 0.0) - jnp.log1p(jnp.exp(-jnp.abs(z)))


def _layer_norm(z, g, b):
    mu = jnp.mean(z, axis=-1, keepdims=True)
    zc = z - mu
    var = jnp.mean(zc * zc, axis=-1, keepdims=True)
    return zc * lax.rsqrt(var + LN_EPS) * g + b


def _ret_log_decay(h):
    return math.log(1.0 - 2.0 ** (-5.0 - h))


def _inproj_weight_kernel(w_ref, wm_ref, wg_ref, wf_ref):
    wm_ref[...] = w_ref[:, :C_FF].astype(BF16)
    wg_ref[...] = w_ref[:, C_GATES:].astype(BF16)
    ff = w_ref[:, C_FF:C_GATES]
    pad = jnp.zeros((ff.shape[0], LANES - FOX_HEADS), F32)
    wf_ref[...] = jnp.concatenate([ff, pad], axis=1).astype(BF16)


def _inproj_weights(w, *, rows):
    d, n = w.shape
    blk = lambda width: pl.BlockSpec((rows, width), lambda i: (i, 0))
    return pl.pallas_call(
        _inproj_weight_kernel,
        grid=(d // rows,),
        in_specs=[blk(n)],
        out_specs=(blk(C_FF), blk(n - C_GATES), blk(LANES)),
        out_shape=(jax.ShapeDtypeStruct((d, C_FF), BF16),
                   jax.ShapeDtypeStruct((d, n - C_GATES), BF16),
                   jax.ShapeDtypeStruct((d, LANES), BF16)),
        compiler_params=pltpu.CompilerParams(dimension_semantics=("parallel",)),
        name="inproj_weights",
    )(w)


def _inproj_kernel(x_ref, w_ref, wg_ref, wf_ref, cos_ref, sin_ref, bf_ref,
                   rq_ref, rk_ref, rv_ref, rg_ref, fq_ref, fk_ref, fv_ref,
                   ga_ref, gb_ref, lf8_ref, lfp_ref, *extra_refs):
    xb = x_ref[...].astype(BF16)

    def mm(lo, hi, ref=w_ref):
        return jnp.dot(xb, ref[:, lo:hi], preferred_element_type=F32)

    cos = cos_ref[...]
    sin = sin_ref[...]

    def rope(v):
        outs = []
        for h in range(RET_HEADS):
            vh = v[:, h * RET_DK:(h + 1) * RET_DK]
            outs.append(vh * cos + pltpu.roll(vh, RET_DK // 2, 1) * sin)
        return jnp.concatenate(outs, axis=1)

    rq_ref[...] = rope(mm(C_RQ, C_RK)).astype(rq_ref.dtype)
    rk_ref[...] = rope(mm(C_RK, C_RV)) * (RET_DK ** -0.5)
    rv_ref[...] = mm(C_RV, C_RG).astype(rv_ref.dtype)
    rg_ref[...] = mm(C_RG, C_FQ)
    fq_ref[...] = mm(C_FQ, C_FK).astype(fq_ref.dtype)
    fk = mm(C_FK, C_FV)
    fk_ref[...] = fk
    fv = mm(C_FV, C_FF)
    fv_ref[...] = fv
    if extra_refs:
        fkb_ref, fvt_ref = extra_refs
        fkb_ref[...] = fk.astype(BF16)
        for h in range(FOX_HEADS):
            fvt_ref[0, h, 0] = fv[:, h * FOX_DH:(h + 1) * FOX_DH].T.astype(BF16)
    ga_ref[...] = mm(0, D_MODEL, wg_ref)
    gb_ref[...] = mm(D_MODEL, 2 * D_MODEL, wg_ref)
    lf = _log_sigmoid(mm(0, LANES, wf_ref) + bf_ref[...])
    lfp_ref[...] = lf
    lf8_ref[...] = lf[:, :FOX_HEADS]


def _inproj(x, weights, cos_t, sin_t, bf_pad, *, tm, act_dtype, prompt_seq=None):
    w_main, w_gates, w_ff = weights
    m = x.shape[0]
    n_tab = cos_t.shape[0] // tm
    row = lambda w: pl.BlockSpec((tm, w), lambda i: (i, 0))
    tab = pl.BlockSpec((tm, LANES), lambda i: (i % n_tab, 0))
    out_shape = [
        jax.ShapeDtypeStruct((m, RET_QK_W), act_dtype),
        jax.ShapeDtypeStruct((m, RET_QK_W), F32),
        jax.ShapeDtypeStruct((m, RET_V_W), act_dtype),
        jax.ShapeDtypeStruct((m, RET_V_W), F32),
        jax.ShapeDtypeStruct((m, FOX_W), act_dtype),
        jax.ShapeDtypeStruct((m, FOX_W), F32),
        jax.ShapeDtypeStruct((m, FOX_W), F32),
        jax.ShapeDtypeStruct((m, D_MODEL), F32),
        jax.ShapeDtypeStruct((m, D_MODEL), F32),
        jax.ShapeDtypeStruct((m, FOX_HEADS), F32),
        jax.ShapeDtypeStruct((m, LANES), F32),
    ]
    out_specs = [row(RET_QK_W), row(RET_QK_W), row(RET_V_W), row(RET_V_W), row(FOX_W),
                 row(FOX_W), row(FOX_W), row(D_MODEL), row(D_MODEL), row(FOX_HEADS), row(LANES)]
    if prompt_seq is not None:
        nt = prompt_seq // tm
        out_shape += [
            jax.ShapeDtypeStruct((m, FOX_W), BF16),
            jax.ShapeDtypeStruct((m // prompt_seq, FOX_HEADS, nt, FOX_DH, tm), BF16),
        ]
        out_specs += [row(FOX_W),
                      pl.BlockSpec((1, FOX_HEADS, 1, FOX_DH, tm), lambda i: (i // nt, 0, i % nt, 0, 0))]
    return pl.pallas_call(
        _inproj_kernel,
        grid=(m // tm,),
        in_specs=[row(D_MODEL), _resident(w_main.shape), _resident(w_gates.shape),
                  _resident(w_ff.shape), tab, tab, _resident(bf_pad.shape)],
        out_specs=out_specs,
        out_shape=out_shape,
        compiler_params=pltpu.CompilerParams(
            dimension_semantics=("parallel",), vmem_limit_bytes=VMEM_LIMIT_LARGE),
        name="inproj",
    )(x, w_main, w_gates, w_ff, cos_t, sin_t, bf_pad)


def _cumsum_kernel(lf_ref, fcol_ref, frow_ref, *, blk):
    t = lf_ref.shape[0]
    ri = lax.broadcasted_iota(jnp.int32, (blk, blk), 0)
    ci = lax.broadcasted_iota(jnp.int32, (blk, blk), 1)
    tri = (ri >= ci).astype(F32)
    carry = jnp.zeros((1, LANES), F32)
    for b in range(t // blk):
        xb = lf_ref[b * blk:(b + 1) * blk, :]
        fb = jnp.dot(tri, xb, precision=HIGHEST, preferred_element_type=F32) + carry
        fcol_ref[b * blk:(b + 1) * blk, :] = fb
        frow_ref[:, b * blk:(b + 1) * blk] = fb.T[:FOX_HEADS, :]
        carry = fb[blk - 1:blk, :]


def _cumsum(lf_pad, *, batch, seq):
    return pl.pallas_call(
        functools.partial(_cumsum_kernel, blk=LANES),
        grid=(batch,),
        in_specs=[pl.BlockSpec((seq, LANES), lambda b: (b, 0))],
        out_specs=(pl.BlockSpec((seq, LANES), lambda b: (b, 0)),
                   pl.BlockSpec((FOX_HEADS, seq), lambda b: (b, 0))),
        out_shape=(jax.ShapeDtypeStruct((batch * seq, LANES), F32),
                   jax.ShapeDtypeStruct((batch * FOX_HEADS, seq), F32)),
        compiler_params=pltpu.CompilerParams(dimension_semantics=("parallel",)),
        name="logf_cumsum",
    )(lf_pad)


def _ret_head_out(o, gain, rg):
    mu = jnp.mean(o, axis=-1, keepdims=True)
    oc = o - mu
    var = jnp.mean(oc * oc, axis=-1, keepdims=True)
    y = oc * lax.rsqrt(var + GN_EPS) * gain
    return y * (rg * jax.nn.sigmoid(rg))


def _ret_prompt_kernel(q_ref, k_ref, v_ref, rg_ref, gain_ref, o_ref, sfin_ref, s_scr, *, chunk):
    c = pl.program_id(1)

    @pl.when(c == 0)
    def _():
        s_scr[...] = jnp.zeros_like(s_scr)

    ii = lax.broadcasted_iota(jnp.int32, (chunk, chunk), 0)
    jj = lax.broadcasted_iota(jnp.int32, (chunk, chunk), 1)
    causal = ii >= jj
    expo = jnp.where(causal, (ii - jj).astype(F32), 0.0)
    ic = lax.broadcasted_iota(jnp.int32, (chunk, 1), 0).astype(F32)
    for h in range(RET_HEADS):
        ld = _ret_log_decay(h)
        dmat = jnp.where(causal, jnp.exp(expo * ld), 0.0)
        cross_decay = jnp.exp((ic + 1.0) * ld)
        k_decay = jnp.exp((chunk - 1.0 - ic) * ld)
        chunk_decay = math.exp(chunk * ld)
        s_run = s_scr[h]
        for cc in range(q_ref.shape[0] // chunk):
            rows = slice(cc * chunk, (cc + 1) * chunk)
            qh = q_ref[rows, h * RET_DK:(h + 1) * RET_DK].astype(BF16)
            kh = k_ref[rows, h * RET_DK:(h + 1) * RET_DK]
            vh = v_ref[rows, h * RET_DV:(h + 1) * RET_DV].astype(BF16)
            sc = lax.dot_general(qh, kh.astype(BF16), NT_DIMS, preferred_element_type=F32) * dmat
            intra = jnp.dot(sc.astype(BF16), vh, preferred_element_type=F32)
            cross = jnp.dot(qh, s_run.astype(BF16), preferred_element_type=F32) * cross_decay
            kd = (kh * k_decay).astype(BF16)
            s_run = s_run * chunk_decay + lax.dot_general(kd, vh, TN_DIMS, preferred_element_type=F32)
            sl = slice(h * RET_DV, (h + 1) * RET_DV)
            o_ref[rows, sl] = _ret_head_out(
                intra + cross, gain_ref[:, sl], rg_ref[rows, sl]).astype(o_ref.dtype)
        s_scr[h] = s_run

    @pl.when(c == pl.num_programs(1) - 1)
    def _():
        sfin_ref[0] = s_scr[...]


def _ret_prompt(rq, rk, rv, rg, gain, *, batch, seq, chunks_per_step):
    rows = CHUNK * chunks_per_step
    nc = seq // rows
    row = lambda w: pl.BlockSpec((rows, w), lambda b, c: (b * nc + c, 0))
    return pl.pallas_call(
        functools.partial(_ret_prompt_kernel, chunk=CHUNK),
        grid=(batch, nc),
        in_specs=[row(RET_QK_W), row(RET_QK_W), row(RET_V_W), row(RET_V_W),
                  pl.BlockSpec((1, RET_V_W), lambda b, c: (0, 0))],
        out_specs=(row(RET_V_W),
                   pl.BlockSpec((1, RET_HEADS, RET_DK, RET_DV), lambda b, c: (b, 0, 0, 0))),
        out_shape=(jax.ShapeDtypeStruct((batch * seq, RET_V_W), BF16),
                   jax.ShapeDtypeStruct((batch, RET_HEADS, RET_DK, RET_DV), F32)),
        scratch_shapes=[pltpu.VMEM((RET_HEADS, RET_DK, RET_DV), F32)],
        compiler_params=pltpu.CompilerParams(
            dimension_semantics=("parallel", "arbitrary"), vmem_limit_bytes=VMEM_LIMIT_MEDIUM),
        name="ret_prompt",
    )(rq, rk, rv, rg, gain)


def _ret_sample_kernel(q_ref, k_ref, v_ref, rg_ref, gain_ref, s0_ref, o_ref, s1_ref, *, group, t_new):
    rows = group * t_new
    shift = t_new.bit_length() - 1
    ri = lax.broadcasted_iota(jnp.int32, (rows, rows), 0)
    ci = lax.broadcasted_iota(jnp.int32, (rows, rows), 1)
    valid = ((ri >> shift) == (ci >> shift)) & (ri >= ci)
    expo = jnp.where(valid, (ri - ci).astype(F32), 0.0)
    r1 = lax.broadcasted_iota(jnp.int32, (rows, 1), 0)
    tpos = (r1 & (t_new - 1)).astype(F32)
    rowgrp = lax.broadcasted_iota(jnp.int32, (rows, RET_DK), 0) >> shift
    for h in range(RET_HEADS):
        ld = _ret_log_decay(h)
        dmat = jnp.where(valid, jnp.exp(expo * ld), 0.0)
        cross_decay = jnp.exp((tpos + 1.0) * ld)
        k_decay = jnp.exp((t_new - 1.0 - tpos) * ld)
        chunk_decay = math.exp(t_new * ld)
        qh = q_ref[:, h * RET_DK:(h + 1) * RET_DK]
        kh = k_ref[:, h * RET_DK:(h + 1) * RET_DK]
        vh = v_ref[:, h * RET_DV:(h + 1) * RET_DV].astype(BF16)
        qb = qh.astype(BF16)
        sc = lax.dot_general(qb, kh.astype(BF16), NT_DIMS, preferred_element_type=F32) * dmat
        o = jnp.dot(sc.astype(BF16), vh, preferred_element_type=F32)
        kd = kh * k_decay
        cross = jnp.zeros((rows, RET_DV), F32)
        for g in range(group):
            s_old = s0_ref[g, h]
            in_g = rowgrp == g
            qg = jnp.where(in_g, qh, 0.0).astype(BF16)
            cross = cross + jnp.dot(qg, s_old.astype(BF16), preferred_element_type=F32)
            kg = jnp.where(in_g, kd, 0.0).astype(BF16)
            s1_ref[g, h] = s_old * chunk_decay + lax.dot_general(
                kg, vh, TN_DIMS, preferred_element_type=F32)
        o = o + cross * cross_decay
        sl = slice(h * RET_DV, (h + 1) * RET_DV)
        o_ref[:, sl] = _ret_head_out(o, gain_ref[:, sl], rg_ref[:, sl]).astype(o_ref.dtype)


def _ret_sample(rq, rk, rv, rg, gain, state, *, t_new, group):
    m = rq.shape[0]
    rows = group * t_new
    row = lambda w: pl.BlockSpec((rows, w), lambda i: (i, 0))
    st = pl.BlockSpec((group, RET_HEADS, RET_DK, RET_DV), lambda i: (i, 0, 0, 0))
    return pl.pallas_call(
        functools.partial(_ret_sample_kernel, group=group, t_new=t_new),
        grid=(m // rows,),
        in_specs=[row(RET_QK_W), row(RET_QK_W), row(RET_V_W), row(RET_V_W),
                  pl.BlockSpec((1, RET_V_W), lambda i: (0, 0)), st],
        out_specs=(row(RET_V_W), st),
        out_shape=(jax.ShapeDtypeStruct((m, RET_V_W), BF16),
                   jax.ShapeDtypeStruct(state.shape, F32)),
        compiler_params=pltpu.CompilerParams(
            dimension_semantics=("parallel",), vmem_limit_bytes=VMEM_LIMIT_MEDIUM),
        name="ret_sample",
    )(rq, rk, rv, rg, gain, state)


def _fox_prompt_kernel(q_ref, k_ref, vt_ref, fcum_ref, frow_ref, o_ref,
                       fk_scr, t_scr, p_scr, acc_scr, *, tile, scale):
    hps = vt_ref.shape[1]
    h0 = pl.program_id(1) * hps
    i = pl.program_id(2)
    seq = k_ref.shape[0]
    vt_tile = vt_ref.shape[-1]
    per = tile // vt_tile
    assert per % 2 == 0

    @pl.when(i == 0)
    def _():
        lane = lax.broadcasted_iota(jnp.int32, (seq, LANES), 1)
        for g in range(hps):
            col = jnp.sum(jnp.where(lane == h0 + g, fcum_ref[...], 0.0), axis=1, keepdims=True)
            fk_scr[g] = jnp.broadcast_to(col * LOG2E, (seq, LANES))

    def scores(g, u):
        start = pl.multiple_of(u * vt_tile, vt_tile)
        ks = k_ref[pl.ds(start, vt_tile), g * FOX_DH:(g + 1) * FOX_DH]
        fk2 = fk_scr[g, pl.ds(start, vt_tile), :]
        t = lax.dot_general(ks, q_ref[:, g * FOX_DH:(g + 1) * FOX_DH], NT_DIMS,
                            preferred_element_type=F32) * (scale * LOG2E)
        return t - jnp.concatenate([fk2] * (tile // LANES), axis=1)

    def sub_step(g, u, slot, state, *, diag_part=None, last=False):
        m_run, l_run, alpha_prev = state
        fq2 = frow_ref[g, pl.ds(i, 1), :] * LOG2E
        if not last:
            t_scr[g, 1 - slot] = scores(g, u + 1)
        pv_prev = jnp.dot(vt_ref[0, g, jnp.maximum(u - 1, 0)], p_scr[g, 1 - slot],
                          preferred_element_type=F32)
        acc_scr[g] = alpha_prev * acc_scr[g] + pv_prev
        t = t_scr[g, slot]
        if diag_part is not None:
            key = diag_part * vt_tile + lax.broadcasted_iota(jnp.int32, (vt_tile, tile), 0)
            qry = lax.broadcasted_iota(jnp.int32, (vt_tile, tile), 1)
            t = jnp.where(key <= qry, t, -jnp.inf)
        m_new = jnp.maximum(m_run, jnp.max(t, axis=0, keepdims=True) + fq2)
        alpha = jnp.exp2(m_run - m_new)
        p = jnp.exp2(t + (fq2 - m_new))
        l_new = alpha * l_run + jnp.sum(p, axis=0, keepdims=True)
        p_scr[g, slot] = p.astype(BF16)
        return m_new, l_new, alpha

    def pair(j, states):
        for s in range(per):
            states = tuple(sub_step(g, j * per + s, s % 2, states[g]) for g in range(hps))
        return states

    for g in range(hps):
        t_scr[g, 0] = scores(g, 0)
        p_scr[g, 1] = jnp.zeros((vt_tile, tile), BF16)
        acc_scr[g] = jnp.zeros((FOX_DH, tile), F32)
    init = (jnp.full((1, tile), -jnp.inf, F32), jnp.zeros((1, tile), F32), jnp.ones((1, tile), F32))
    states = lax.fori_loop(0, i, pair, (init,) * hps)
    for s in range(per):
        states = tuple(sub_step(g, i * per + s, s % 2, states[g], diag_part=s, last=(s == per - 1))
                       for g in range(hps))
    last_u = i * per + per - 1
    for g in range(hps):
        _, l_fin, alpha_prev = states[g]
        acc = alpha_prev * acc_scr[g] + jnp.dot(vt_ref[0, g, last_u], p_scr[g, (per - 1) % 2],
                                         preferred_element_type=F32)
        o_ref[:, g * FOX_DH:(g + 1) * FOX_DH] = (acc / l_fin).T.astype(o_ref.dtype)


def _fox_prompt(fq, fkb, fvt, fcum, frow, *, batch, seq, tile, heads_per_step):
    nq = seq // tile
    hps = heads_per_step
    frow3 = frow.reshape(batch * FOX_HEADS, nq, tile)
    nt, vt_tile = fvt.shape[2], fvt.shape[4]
    return pl.pallas_call(
        functools.partial(_fox_prompt_kernel, tile=tile, scale=FOX_DH ** -0.5),
        grid=(batch, FOX_HEADS // hps, nq),
        in_specs=[pl.BlockSpec((tile, hps * FOX_DH), lambda b, h, i: (b * nq + i, h)),
                  pl.BlockSpec((seq, hps * FOX_DH), lambda b, h, i: (b, h)),
                  pl.BlockSpec((1, hps, nt, FOX_DH, vt_tile), lambda b, h, i: (b, h, 0, 0, 0)),
                  pl.BlockSpec((seq, LANES), lambda b, h, i: (b, 0)),
                  pl.BlockSpec((hps, nq, tile), lambda b, h, i: (b * (FOX_HEADS // hps) + h, 0, 0))],
        out_specs=pl.BlockSpec((tile, hps * FOX_DH), lambda b, h, i: (b * nq + i, h)),
        out_shape=jax.ShapeDtypeStruct((batch * seq, FOX_W), BF16),
        scratch_shapes=[pltpu.VMEM((hps, seq, LANES), F32),
                        pltpu.VMEM((hps, 2, vt_tile, tile), F32),
                        pltpu.VMEM((hps, 2, vt_tile, tile), BF16),
                        pltpu.VMEM((hps, FOX_DH, tile), F32)],
        compiler_params=pltpu.CompilerParams(
            dimension_semantics=("parallel", "parallel", "arbitrary")),
        name="fox_prompt",
    )(fq, fkb, fvt, fcum, frow3)


def _fox_sample_kernel(pt_ref, q_ref, kn_ref, vn_ref, ln_ref, *refs, n_pages, scale, page_cols):
    del pt_ref
    kc_refs = refs[:n_pages]
    vc_refs = refs[n_pages:2 * n_pages]
    lc_refs = refs[2 * n_pages:3 * n_pages]
    o_ref, s_scr = refs[3 * n_pages:]
    t_new = q_ref.shape[0]
    assert t_new == SUBLANES
    n_rows = FOX_HEADS * t_new
    tshift = t_new.bit_length() - 1
    hmask = FOX_HEADS - 1
    li = lax.broadcasted_iota(jnp.int32, (LANES, LANES), 0)
    lj = lax.broadcasted_iota(jnp.int32, (LANES, LANES), 1)
    same_head = ((li - lj) & hmask) == 0

    def stack_heads(ref):
        return jnp.concatenate([ref[:, h * FOX_DH:(h + 1) * FOX_DH] for h in range(FOX_HEADS)], axis=0)

    prefix = (((li >> tshift) == (lj >> tshift)) & (li <= lj)).astype(F32)
    xn = jnp.broadcast_to(ln_ref[0], (SUBLANES, LANES))
    fn_row = jnp.dot(xn, prefix, precision=HIGHEST, preferred_element_type=F32)[0:1, :]
    eye = (lax.broadcasted_iota(jnp.int32, (n_rows, LANES), 0)
           == lax.broadcasted_iota(jnp.int32, (n_rows, LANES), 1))
    fq = jnp.sum(jnp.where(eye, fn_row, 0.0), axis=1, keepdims=True)

    assert n_pages * lc_refs[0].shape[1] == LANES
    x = jnp.concatenate([lc_refs[pg][0] for pg in range(n_pages)], axis=0)
    within = jnp.dot(x, (same_head & (li > lj)).astype(F32), precision=HIGHEST,
                     preferred_element_type=F32)
    tot = jnp.dot(x, same_head.astype(F32), precision=HIGHEST, preferred_element_type=F32)
    later_rows = jnp.dot((lj > li).astype(F32), tot, precision=HIGHEST, preferred_element_type=F32)
    g2 = (within + later_rows) * (-LOG2E)
    rows_per_page = lc_refs[0].shape[1]
    g_rows = [jnp.concatenate([g2[pg * rows_per_page + r:pg * rows_per_page + r + 1, :]
                               for r in range(rows_per_page)], axis=1)
              for pg in range(n_pages)]

    c_qk = scale * LOG2E
    fq2 = fq * LOG2E
    qb = stack_heads(q_ref).astype(BF16)
    ri = lax.broadcasted_iota(jnp.int32, (n_rows, page_cols), 0)
    ci = lax.broadcasted_iota(jnp.int32, (n_rows, page_cols), 1)
    head_ok = (ri >> tshift) == (ci & hmask)
    mx = jnp.full((n_rows, 1), -jnp.inf, F32)
    for pg in range(n_pages):
        s = lax.dot_general(qb, kc_refs[pg][0].astype(BF16), NT_DIMS, preferred_element_type=F32)
        s = jnp.where(head_ok, s * c_qk - g_rows[pg], -jnp.inf)
        s_scr[pg] = s
        mx = jnp.maximum(mx, jnp.max(s, axis=1, keepdims=True))

    zpad = jnp.zeros((LANES - n_rows, FOX_DH), BF16)
    kn = jnp.concatenate([stack_heads(kn_ref).astype(BF16), zpad], axis=0)
    vn = jnp.concatenate([stack_heads(vn_ref).astype(BF16), zpad], axis=0)
    r2 = lax.broadcasted_iota(jnp.int32, (n_rows, LANES), 0)
    c2 = lax.broadcasted_iota(jnp.int32, (n_rows, LANES), 1)
    ok = ((r2 >> tshift) == (c2 >> tshift)) & (c2 <= r2)
    s2 = lax.dot_general(qb, kn, NT_DIMS, preferred_element_type=F32)
    s2 = jnp.where(ok, s2 * c_qk - fn_row * LOG2E, -jnp.inf)
    mx = jnp.maximum(mx, jnp.max(s2, axis=1, keepdims=True))

    shift = fq2 - (mx + fq2)
    l = jnp.zeros((n_rows, 1), F32)
    acc = jnp.zeros((n_rows, FOX_DH), F32)
    for pg in range(n_pages):
        pr = jnp.exp2(s_scr[pg] + shift)
        l = l + jnp.sum(pr, axis=1, keepdims=True)
        acc = acc + jnp.dot(pr.astype(BF16), vc_refs[pg][0].astype(BF16), preferred_element_type=F32)
    p2 = jnp.exp2(s2 + shift)
    l = l + jnp.sum(p2, axis=1, keepdims=True)
    acc = acc + jnp.dot(p2.astype(BF16), vn, preferred_element_type=F32)
    out = acc / l
    for h in range(FOX_HEADS):
        o_ref[:, h * FOX_DH:(h + 1) * FOX_DH] = out[h * t_new:(h + 1) * t_new, :]


def _page_index(b, pt, *, j):
    return (pt[b, j], 0, 0)


def _fox_sample(page_table, fq, fk, fv, lf_new, cache_k, cache_v, cache_lf, *, t_new):
    dec_batch, n_pages = page_table.shape
    n_phys, page_size = cache_k.shape[0], cache_k.shape[1]
    n_rows = t_new * FOX_HEADS
    page_cols = page_size * FOX_HEADS
    ln = lf_new.reshape(dec_batch, t_new, FOX_HEADS).transpose(0, 2, 1).reshape(dec_batch, 1, n_rows)
    ln = jnp.pad(ln, ((0, 0), (0, 0), (0, LANES - n_rows)))
    kc = cache_k.reshape(n_phys, page_cols, FOX_DH)
    vc = cache_v.reshape(n_phys, page_cols, FOX_DH)
    lc = cache_lf.reshape(n_phys, page_cols // LANES, LANES)
    new_spec = pl.BlockSpec((t_new, FOX_W), lambda b, pt: (b, 0))
    kv_specs = [pl.BlockSpec((1, page_cols, FOX_DH), functools.partial(_page_index, j=j))
                for j in range(n_pages)]
    lf_specs = [pl.BlockSpec((1, page_cols // LANES, LANES), functools.partial(_page_index, j=j))
                for j in range(n_pages)]
    grid_spec = pltpu.PrefetchScalarGridSpec(
        num_scalar_prefetch=1,
        grid=(dec_batch,),
        in_specs=[new_spec, new_spec, new_spec, pl.BlockSpec((1, 1, LANES), lambda b, pt: (b, 0, 0))]
        + kv_specs + kv_specs + lf_specs,
        out_specs=new_spec,
        scratch_shapes=[pltpu.VMEM((n_pages, n_rows, page_cols), F32)],
    )
    return pl.pallas_call(
        functools.partial(_fox_sample_kernel, n_pages=n_pages, scale=FOX_DH ** -0.5,
                          page_cols=page_cols),
        grid_spec=grid_spec,
        out_shape=jax.ShapeDtypeStruct((dec_batch * t_new, FOX_W), F32),
        compiler_params=pltpu.CompilerParams(
            dimension_semantics=("parallel",), vmem_limit_bytes=VMEM_LIMIT_LARGE),
        name="fox_sample",
    )(page_table, fq, fk, fv, ln, *([kc] * n_pages), *([vc] * n_pages), *([lc] * n_pages))


def _merge_kernel(ro_ref, fo_ref, ga_ref, gb_ref, x_ref, wr_ref, wf_ref, wo_ref, g_ref, b_ref, h_ref):
    a = jnp.dot(ro_ref[...].astype(BF16), wr_ref[...], preferred_element_type=F32)
    b = jnp.dot(fo_ref[...].astype(BF16), wf_ref[...], preferred_element_type=F32)
    merged = jax.nn.sigmoid(ga_ref[...]) * a + jax.nn.sigmoid(gb_ref[...]) * b
    mix = jnp.dot(merged.astype(BF16), wo_ref[...], preferred_element_type=F32)
    h_ref[...] = _layer_norm(ALPHA * x_ref[...] + mix, g_ref[...], b_ref[...])


def _merge(ret_o, fox_o, ga, gb, x, wr, wf, wo, g, b, *, tm):
    m = x.shape[0]
    row = lambda w: pl.BlockSpec((tm, w), lambda i: (i, 0))
    return pl.pallas_call(
        _merge_kernel,
        grid=(m // tm,),
        in_specs=[row(RET_V_W), row(FOX_W), row(D_MODEL), row(D_MODEL), row(D_MODEL),
                  _resident(wr.shape), _resident(wf.shape), _resident(wo.shape),
                  _resident(g.shape), _resident(b.shape)],
        out_specs=row(D_MODEL),
        out_shape=jax.ShapeDtypeStruct((m, D_MODEL), F32),
        compiler_params=pltpu.CompilerParams(
            dimension_semantics=("parallel",), vmem_limit_bytes=VMEM_LIMIT_MEDIUM),
        name="merge_out_ln",
    )(ret_o, fox_o, ga, gb, x, wr, wf, wo, g, b)


def _ffn_kernel(h_ref, wu_ref, wd_ref, g_ref, b_ref, y_ref, *, ff_tile):
    h = h_ref[...]
    hb = h.astype(BF16)
    acc = jnp.zeros(h.shape, F32)
    for c in range(D_FF // ff_tile):
        sl = slice(c * ff_tile, (c + 1) * ff_tile)
        u = jnp.maximum(jnp.dot(hb, wu_ref[:, sl], preferred_element_type=F32), 0.0)
        acc = acc + jnp.dot((u * u).astype(BF16), wd_ref[sl, :], preferred_element_type=F32)
    y_ref[...] = _layer_norm(ALPHA * h + acc, g_ref[...], b_ref[...])


def _ffn(h, wu, wd, g, b, *, tm, ff_tile):
    m = h.shape[0]
    row = pl.BlockSpec((tm, D_MODEL), lambda i: (i, 0))
    return pl.pallas_call(
        functools.partial(_ffn_kernel, ff_tile=ff_tile),
        grid=(m // tm,),
        in_specs=[row, _resident(wu.shape), _resident(wd.shape), _resident(g.shape),
                  _resident(b.shape)],
        out_specs=row,
        out_shape=jax.ShapeDtypeStruct((m, D_MODEL), F32),
        compiler_params=pltpu.CompilerParams(
            dimension_semantics=("parallel",), vmem_limit_bytes=VMEM_LIMIT_MEDIUM),
        name="ffn_ln",
    )(h, wu, wd, g, b)


def _rope_tables(pos):
    half = RET_DK // 2
    inv_freq = ROPE_BASE ** (-np.arange(half, dtype=np.float64) / half)
    ang = np.asarray(pos, np.float64)[:, None] * inv_freq[None, :]
    c, s = np.cos(ang), np.sin(ang)
    return (jnp.asarray(np.concatenate([c, c], axis=1), F32),
            jnp.asarray(np.concatenate([-s, s], axis=1), F32))


def kernel(x_prompt, x_sample, cache_k, cache_v, cache_logf, state_ret, page_table, w_in, b_forget,
           ret_gn_gain, w_ret_proj, w_fox_proj, w_out, ln1_g, ln1_b, w_ff_up, w_ff_down, ln2_g,
           ln2_b):
    batch, seq, _ = x_prompt.shape
    dec_batch, t_new, _ = x_sample.shape
    page_size = cache_k.shape[2]
    past_len = page_table.shape[1] * page_size
    tm = ROW_TILE
    assert seq % ATTN_QUERY_TILE == 0 and seq % (CHUNK * RET_CHUNKS_PER_STEP) == 0
    assert (batch * seq) % TAIL_ROW_TILE == 0 and (dec_batch * t_new) % TAIL_ROW_TILE == 0
    assert dec_batch % RET_SAMPLE_GROUP == 0 and tm % t_new == 0

    w_pack = _inproj_weights(w_in[0], rows=LANES)
    bf_pad = jnp.pad(b_forget[0].astype(F32), (0, LANES - FOX_HEADS)).reshape(1, LANES)
    gain = ret_gn_gain[0].reshape(1, RET_V_W)
    wr, wf, wo = w_ret_proj[0].astype(BF16), w_fox_proj[0].astype(BF16), w_out[0].astype(BF16)
    wu, wd = w_ff_up[0].astype(BF16), w_ff_down[0].astype(BF16)
    g1, b1 = ln1_g[0].reshape(1, D_MODEL), ln1_b[0].reshape(1, D_MODEL)
    g2, b2 = ln2_g[0].reshape(1, D_MODEL), ln2_b[0].reshape(1, D_MODEL)

    def tail(ret_o, fox_o, ga, gb, x2):
        h = _merge(ret_o, fox_o, ga, gb, x2, wr, wf, wo, g1, b1, tm=TAIL_ROW_TILE)
        return _ffn(h, wu, wd, g2, b2, tm=TAIL_ROW_TILE, ff_tile=D_FF)

    xp = x_prompt.reshape(batch * seq, D_MODEL)
    cos_p, sin_p = _rope_tables(np.arange(seq))
    (rq, rk, rv, rg, fq, fk, fv, ga, gb, lf8, lfp, fkb, fvt) = _inproj(
        xp, w_pack, cos_p, sin_p, bf_pad, tm=tm, act_dtype=BF16, prompt_seq=seq)
    fcum, frow = _cumsum(lfp, batch=batch, seq=seq)
    ret_o, s_prompt = _ret_prompt(rq, rk, rv, rg, gain, batch=batch, seq=seq,
                                  chunks_per_step=RET_CHUNKS_PER_STEP)
    fox_o = _fox_prompt(fq, fkb, fvt, fcum, frow, batch=batch, seq=seq, tile=ATTN_QUERY_TILE,
                        heads_per_step=ATTN_HEADS_PER_STEP)
    y_prompt = tail(ret_o, fox_o, ga, gb, xp).reshape(batch, seq, D_MODEL)

    xs = x_sample.reshape(dec_batch * t_new, D_MODEL)
    cos_s, sin_s = _rope_tables(past_len + np.arange(tm) % t_new)
    (rq_s, rk_s, rv_s, rg_s, fq_s, fk_s, fv_s, ga_s, gb_s, lf8_s, _) = _inproj(
        xs, w_pack, cos_s, sin_s, bf_pad, tm=tm, act_dtype=F32)
    ret_o_s, s_sample = _ret_sample(rq_s, rk_s, rv_s, rg_s, gain, state_ret[0], t_new=t_new,
                                    group=RET_SAMPLE_GROUP)
    fox_o_s = _fox_sample(page_table, fq_s, fk_s, fv_s, lf8_s, cache_k[0], cache_v[0], cache_logf[0],
                          t_new=t_new)
    y_sample = tail(ret_o_s, fox_o_s, ga_s, gb_s, xs).reshape(dec_batch, t_new, D_MODEL)

    return (y_prompt, y_sample,
            fk.reshape(1, batch, seq, FOX_HEADS, FOX_DH), fv.reshape(1, batch, seq, FOX_HEADS, FOX_DH),
            lf8.reshape(1, batch, seq, FOX_HEADS), s_prompt[None],
            fk_s.reshape(1, dec_batch, t_new, FOX_HEADS, FOX_DH),
            fv_s.reshape(1, dec_batch, t_new, FOX_HEADS, FOX_DH),
            lf8_s.reshape(1, dec_batch, t_new, FOX_HEADS), s_sample[None])
```

```python
import functools
import math

import jax
import jax.numpy as jnp
import numpy as np
from jax import lax
from jax.experimental import pallas as pl
from jax.experimental.pallas import tpu as pltpu

F32 = jnp.float32
BF16 = jnp.bfloat16
HIGHEST = lax.Precision.HIGHEST

D_MODEL = 1024
RET_HEADS = 4
RET_DK = 128
RET_DV = 256
FOX_HEADS = 8
FOX_DH = 128
D_FF = 4 * D_MODEL
CHUNK = 128
ROPE_BASE = 10000.0
LN_EPS = 1e-5
GN_EPS = 1e-6
DEPTH = 1
ALPHA = (2.0 * DEPTH) ** 0.25
RET_QK_W = RET_HEADS * RET_DK
RET_V_W = RET_HEADS * RET_DV
FOX_W = FOX_HEADS * FOX_DH
LOG2E = math.log2(math.e)
LANES = 128
SUBLANES = 8
VMEM_V7X_BYTES = 64 * 1024 * 1024
VMEM_LIMIT_LARGE = VMEM_V7X_BYTES * 7 // 8
VMEM_LIMIT_MEDIUM = VMEM_V7X_BYTES * 3 // 4

ROW_TILE = 256
TAIL_ROW_TILE = 512
ATTN_QUERY_TILE = 512
ATTN_HEADS_PER_STEP = 4
RET_CHUNKS_PER_STEP = 8
RET_SAMPLE_GROUP = 16

NT_DIMS = (((1,), (1,)), ((), ()))
TN_DIMS = (((0,), (0,)), ((), ()))

C_RQ = 0
C_RK = C_RQ + RET_QK_W
C_RV = C_RK + RET_QK_W
C_RG = C_RV + RET_V_W
C_FQ = C_RG + RET_V_W
C_FK = C_FQ + FOX_W
C_FV = C_FK + FOX_W
C_FF = C_FV + FOX_W
C_GATES = C_FF + FOX_HEADS


def _resident(shape):
    return pl.BlockSpec(shape, lambda *_: (0,) * len(shape), pipeline_mode=pl.Buffered(1))


def _log_sigmoid(z):
    return jnp.minimum(z, 0.0) - jnp.log1p(jnp.exp(-jnp.abs(z)))


def _layer_norm(z, g, b):
    mu = jnp.mean(z, axis=-1, keepdims=True)
    zc = z - mu
    var = jnp.mean(zc * zc, axis=-1, keepdims=True)
    return zc * lax.rsqrt(var + LN_EPS) * g + b


def _ret_log_decay(h):
    return math.log(1.0 - 2.0 ** (-5.0 - h))


def _inproj_kernel(x_ref, w_ref, wg_ref, wf_ref, cos_ref, sin_ref, bf_ref,
                   rq_ref, rk_ref, rv_ref, rg_ref, fq_ref, fk_ref, fv_ref,
                   ga_ref, gb_ref, lf8_ref, lfp_ref, *extra_refs):
    xb = x_ref[...].astype(BF16)

    def mm(lo, hi, ref=w_ref):
        return lax.dot_general(xb, ref[lo:hi, :], NT_DIMS, preferred_element_type=F32)

    cos = cos_ref[...]
    sin = sin_ref[...]

    def rope(v):
        outs = []
        for h in range(RET_HEADS):
            vh = v[:, h * RET_DK:(h + 1) * RET_DK]
            outs.append(vh * cos + pltpu.roll(vh, RET_DK // 2, 1) * sin)
        return jnp.concatenate(outs, axis=1)

    rq_ref[...] = rope(mm(C_RQ, C_RK)).astype(rq_ref.dtype)
    rk_ref[...] = rope(mm(C_RK, C_RV)) * (RET_DK ** -0.5)
    rv_ref[...] = mm(C_RV, C_RG).astype(rv_ref.dtype)
    rg_ref[...] = mm(C_RG, C_FQ)
    fq_ref[...] = mm(C_FQ, C_FK).astype(fq_ref.dtype)
    fk = mm(C_FK, C_FV)
    fk_ref[...] = fk
    fv = mm(C_FV, C_FF)
    fv_ref[...] = fv
    if extra_refs:
        fkb_ref, fvt_ref = extra_refs
        fkb_ref[...] = fk.astype(BF16)
        for h in range(FOX_HEADS):
            fvt_ref[0, h, 0] = fv[:, h * FOX_DH:(h + 1) * FOX_DH].T.astype(BF16)
    ga_ref[...] = mm(0, D_MODEL, wg_ref)
    gb_ref[...] = mm(D_MODEL, 2 * D_MODEL, wg_ref)
    lf = _log_sigmoid(mm(0, LANES, wf_ref) + bf_ref[...])
    lfp_ref[...] = lf
    lf8_ref[...] = lf[:, :FOX_HEADS]


def _inproj(x, weights, cos_t, sin_t, bf_pad, *, tm, act_dtype, prompt_seq=None):
    w_main, w_gates, w_ff = weights
    m = x.shape[0]
    n_tab = cos_t.shape[0] // tm
    row = lambda w: pl.BlockSpec((tm, w), lambda i: (i, 0))
    tab = pl.BlockSpec((tm, LANES), lambda i: (i % n_tab, 0))
    out_shape = [
        jax.ShapeDtypeStruct((m, RET_QK_W), act_dtype),
        jax.ShapeDtypeStruct((m, RET_QK_W), F32),
        jax.ShapeDtypeStruct((m, RET_V_W), act_dtype),
        jax.ShapeDtypeStruct((m, RET_V_W), F32),
        jax.ShapeDtypeStruct((m, FOX_W), act_dtype),
        jax.ShapeDtypeStruct((m, FOX_W), F32),
        jax.ShapeDtypeStruct((m, FOX_W), F32),
        jax.ShapeDtypeStruct((m, D_MODEL), F32),
        jax.ShapeDtypeStruct((m, D_MODEL), F32),
        jax.ShapeDtypeStruct((m, FOX_HEADS), F32),
        jax.ShapeDtypeStruct((m, LANES), F32),
    ]
    out_specs = [row(RET_QK_W), row(RET_QK_W), row(RET_V_W), row(RET_V_W), row(FOX_W),
                 row(FOX_W), row(FOX_W), row(D_MODEL), row(D_MODEL), row(FOX_HEADS), row(LANES)]
    if prompt_seq is not None:
        nt = prompt_seq // tm
        out_shape += [
            jax.ShapeDtypeStruct((m, FOX_W), BF16),
            jax.ShapeDtypeStruct((m // prompt_seq, FOX_HEADS, nt, FOX_DH, tm), BF16),
        ]
        out_specs += [row(FOX_W),
                      pl.BlockSpec((1, FOX_HEADS, 1, FOX_DH, tm), lambda i: (i // nt, 0, i % nt, 0, 0))]
    return pl.pallas_call(
        _inproj_kernel,
        grid=(m // tm,),
        in_specs=[row(D_MODEL), _resident(w_main.shape), _resident(w_gates.shape),
                  _resident(w_ff.shape), tab, tab, _resident(bf_pad.shape)],
        out_specs=out_specs,
        out_shape=out_shape,
        compiler_params=pltpu.CompilerParams(
            dimension_semantics=("parallel",), vmem_limit_bytes=VMEM_LIMIT_LARGE),
        name="inproj",
    )(x, w_main, w_gates, w_ff, cos_t, sin_t, bf_pad)


def _cumsum_kernel(lf_ref, fcol_ref, frow_ref, *, blk):
    t = lf_ref.shape[0]
    ri = lax.broadcasted_iota(jnp.int32, (blk, blk), 0)
    ci = lax.broadcasted_iota(jnp.int32, (blk, blk), 1)
    tri = (ri >= ci).astype(F32)
    carry = jnp.zeros((1, LANES), F32)
    for b in range(t // blk):
        xb = lf_ref[b * blk:(b + 1) * blk, :]
        fb = jnp.dot(tri, xb, precision=HIGHEST, preferred_element_type=F32) + carry
        fcol_ref[b * blk:(b + 1) * blk, :] = fb
        frow_ref[:, b * blk:(b + 1) * blk] = fb.T[:FOX_HEADS, :]
        carry = fb[blk - 1:blk, :]


def _cumsum(lf_pad, *, batch, seq):
    return pl.pallas_call(
        functools.partial(_cumsum_kernel, blk=LANES),
        grid=(batch,),
        in_specs=[pl.BlockSpec((seq, LANES), lambda b: (b, 0))],
        out_specs=(pl.BlockSpec((seq, LANES), lambda b: (b, 0)),
                   pl.BlockSpec((FOX_HEADS, seq), lambda b: (b, 0))),
        out_shape=(jax.ShapeDtypeStruct((batch * seq, LANES), F32),
                   jax.ShapeDtypeStruct((batch * FOX_HEADS, seq), F32)),
        compiler_params=pltpu.CompilerParams(dimension_semantics=("parallel",)),
        name="logf_cumsum",
    )(lf_pad)


def _ret_head_out(o, gain, rg):
    mu = jnp.mean(o, axis=-1, keepdims=True)
    oc = o - mu
    var = jnp.mean(oc * oc, axis=-1, keepdims=True)
    y = oc * lax.rsqrt(var + GN_EPS) * gain
    return y * (rg * jax.nn.sigmoid(rg))


def _ret_prompt_kernel(q_ref, k_ref, v_ref, rg_ref, gain_ref, o_ref, sfin_ref, s_scr, *, chunk):
    c = pl.program_id(1)

    @pl.when(c == 0)
    def _():
        s_scr[...] = jnp.zeros_like(s_scr)

    ii = lax.broadcasted_iota(jnp.int32, (chunk, chunk), 0)
    jj = lax.broadcasted_iota(jnp.int32, (chunk, chunk), 1)
    causal = ii >= jj
    expo = jnp.where(causal, (ii - jj).astype(F32), 0.0)
    ic = lax.broadcasted_iota(jnp.int32, (chunk, 1), 0).astype(F32)
    for h in range(RET_HEADS):
        ld = _ret_log_decay(h)
        dmat = jnp.where(causal, jnp.exp(expo * ld), 0.0)
        cross_decay = jnp.exp((ic + 1.0) * ld)
        k_decay = jnp.exp((chunk - 1.0 - ic) * ld)
        chunk_decay = math.exp(chunk * ld)
        s_run = s_scr[h]
        for cc in range(q_ref.shape[0] // chunk):
            rows = slice(cc * chunk, (cc + 1) * chunk)
            qh = q_ref[rows, h * RET_DK:(h + 1) * RET_DK].astype(BF16)
            kh = k_ref[rows, h * RET_DK:(h + 1) * RET_DK]
            vh = v_ref[rows, h * RET_DV:(h + 1) * RET_DV].astype(BF16)
            sc = lax.dot_general(qh, kh.astype(BF16), NT_DIMS, preferred_element_type=F32) * dmat
            intra = jnp.dot(sc.astype(BF16), vh, preferred_element_type=F32)
            cross = jnp.dot(qh, s_run.astype(BF16), preferred_element_type=F32) * cross_decay
            kd = (kh * k_decay).astype(BF16)
            s_run = s_run * chunk_decay + lax.dot_general(kd, vh, TN_DIMS, preferred_element_type=F32)
            sl = slice(h * RET_DV, (h + 1) * RET_DV)
            o_ref[rows, sl] = _ret_head_out(
                intra + cross, gain_ref[:, sl], rg_ref[rows, sl]).astype(o_ref.dtype)
        s_scr[h] = s_run

    @pl.when(c == pl.num_programs(1) - 1)
    def _():
        sfin_ref[0] = s_scr[...]


def _ret_prompt(rq, rk, rv, rg, gain, *, batch, seq, chunks_per_step):
    rows = CHUNK * chunks_per_step
    nc = seq // rows
    row = lambda w: pl.BlockSpec((rows, w), lambda b, c: (b * nc + c, 0))
    return pl.pallas_call(
        functools.partial(_ret_prompt_kernel, chunk=CHUNK),
        grid=(batch, nc),
        in_specs=[row(RET_QK_W), row(RET_QK_W), row(RET_V_W), row(RET_V_W),
                  pl.BlockSpec((1, RET_V_W), lambda b, c: (0, 0))],
        out_specs=(row(RET_V_W),
                   pl.BlockSpec((1, RET_HEADS, RET_DK, RET_DV), lambda b, c: (b, 0, 0, 0))),
        out_shape=(jax.ShapeDtypeStruct((batch * seq, RET_V_W), BF16),
                   jax.ShapeDtypeStruct((batch, RET_HEADS, RET_DK, RET_DV), F32)),
        scratch_shapes=[pltpu.VMEM((RET_HEADS, RET_DK, RET_DV), F32)],
        compiler_params=pltpu.CompilerParams(
            dimension_semantics=("parallel", "arbitrary"), vmem_limit_bytes=VMEM_LIMIT_MEDIUM),
        name="ret_prompt",
    )(rq, rk, rv, rg, gain)


def _ret_sample_kernel(q_ref, k_ref, v_ref, rg_ref, gain_ref, s0_ref, o_ref, s1_ref, *, group, t_new):
    rows = group * t_new
    shift = t_new.bit_length() - 1
    ri = lax.broadcasted_iota(jnp.int32, (rows, rows), 0)
    ci = lax.broadcasted_iota(jnp.int32, (rows, rows), 1)
    valid = ((ri >> shift) == (ci >> shift)) & (ri >= ci)
    expo = jnp.where(valid, (ri - ci).astype(F32), 0.0)
    r1 = lax.broadcasted_iota(jnp.int32, (rows, 1), 0)
    tpos = (r1 & (t_new - 1)).astype(F32)
    rowgrp = lax.broadcasted_iota(jnp.int32, (rows, RET_DK), 0) >> shift
    for h in range(RET_HEADS):
        ld = _ret_log_decay(h)
        dmat = jnp.where(valid, jnp.exp(expo * ld), 0.0)
        cross_decay = jnp.exp((tpos + 1.0) * ld)
        k_decay = jnp.exp((t_new - 1.0 - tpos) * ld)
        chunk_decay = math.exp(t_new * ld)
        qh = q_ref[:, h * RET_DK:(h + 1) * RET_DK]
        kh = k_ref[:, h * RET_DK:(h + 1) * RET_DK]
        vh = v_ref[:, h * RET_DV:(h + 1) * RET_DV].astype(BF16)
        qb = qh.astype(BF16)
        sc = lax.dot_general(qb, kh.astype(BF16), NT_DIMS, preferred_element_type=F32) * dmat
        o = jnp.dot(sc.astype(BF16), vh, preferred_element_type=F32)
        kd = kh * k_decay
        cross = jnp.zeros((rows, RET_DV), F32)
        for g in range(group):
            s_old = s0_ref[g, h]
            in_g = rowgrp == g
            qg = jnp.where(in_g, qh, 0.0).astype(BF16)
            cross = cross + jnp.dot(qg, s_old.astype(BF16), preferred_element_type=F32)
            kg = jnp.where(in_g, kd, 0.0).astype(BF16)
            s1_ref[g, h] = s_old * chunk_decay + lax.dot_general(
                kg, vh, TN_DIMS, preferred_element_type=F32)
        o = o + cross * cross_decay
        sl = slice(h * RET_DV, (h + 1) * RET_DV)
        o_ref[:, sl] = _ret_head_out(o, gain_ref[:, sl], rg_ref[:, sl]).astype(o_ref.dtype)


def _ret_sample(rq, rk, rv, rg, gain, state, *, t_new, group):
    m = rq.shape[0]
    rows = group * t_new
    row = lambda w: pl.BlockSpec((rows, w), lambda i: (i, 0))
    st = pl.BlockSpec((group, RET_HEADS, RET_DK, RET_DV), lambda i: (i, 0, 0, 0))
    return pl.pallas_call(
        functools.partial(_ret_sample_kernel, group=group, t_new=t_new),
        grid=(m // rows,),
        in_specs=[row(RET_QK_W), row(RET_QK_W), row(RET_V_W), row(RET_V_W),
                  pl.BlockSpec((1, RET_V_W), lambda i: (0, 0)), st],
        out_specs=(row(RET_V_W), st),
        out_shape=(jax.ShapeDtypeStruct((m, RET_V_W), BF16),
                   jax.ShapeDtypeStruct(state.shape, F32)),
        compiler_params=pltpu.CompilerParams(
            dimension_semantics=("parallel",), vmem_limit_bytes=VMEM_LIMIT_MEDIUM),
        name="ret_sample",
    )(rq, rk, rv, rg, gain, state)


def _fox_prompt_kernel(q_ref, k_ref, vt_ref, fcum_ref, frow_ref, o_ref,
                       fk_scr, t_scr, p_scr, acc_scr, *, tile, scale):
    hps = vt_ref.shape[1]
    h0 = pl.program_id(1) * hps
    i = pl.program_id(2)
    seq = k_ref.shape[0]
    vt_tile = vt_ref.shape[-1]
    per = tile // vt_tile
    assert per % 2 == 0

    @pl.when(i == 0)
    def _():
        lane = lax.broadcasted_iota(jnp.int32, (seq, LANES), 1)
        for g in range(hps):
            col = jnp.sum(jnp.where(lane == h0 + g, fcum_ref[...], 0.0), axis=1, keepdims=True)
            fk_scr[g] = jnp.broadcast_to(col * LOG2E, (seq, LANES))

    def scores(g, u):
        start = pl.multiple_of(u * vt_tile, vt_tile)
        ks = k_ref[pl.ds(start, vt_tile), g * FOX_DH:(g + 1) * FOX_DH]
        fk2 = fk_scr[g, pl.ds(start, vt_tile), :]
        t = lax.dot_general(ks, q_ref[:, g * FOX_DH:(g + 1) * FOX_DH], NT_DIMS,
                            preferred_element_type=F32) * (scale * LOG2E)
        return t - jnp.concatenate([fk2] * (tile // LANES), axis=1)

    def sub_step(g, u, slot, state, *, diag_part=None, last=False):
        m_run, l_run, alpha_prev = state
        fq2 = frow_ref[g, pl.ds(i, 1), :] * LOG2E
        if not last:
            t_scr[g, 1 - slot] = scores(g, u + 1)
        pv_prev = jnp.dot(vt_ref[0, g, jnp.maximum(u - 1, 0)], p_scr[g, 1 - slot],
                          preferred_element_type=F32)
        acc_scr[g] = alpha_prev * acc_scr[g] + pv_prev
        t = t_scr[g, slot]
        if diag_part is not None:
            key = diag_part * vt_tile + lax.broadcasted_iota(jnp.int32, (vt_tile, tile), 0)
            qry = lax.broadcasted_iota(jnp.int32, (vt_tile, tile), 1)
            t = jnp.where(key <= qry, t, -jnp.inf)
        m_new = jnp.maximum(m_run, jnp.max(t, axis=0, keepdims=True) + fq2)
        alpha = jnp.exp2(m_run - m_new)
        p = jnp.exp2(t + (fq2 - m_new))
        l_new = alpha * l_run + jnp.sum(p, axis=0, keepdims=True)
        p_scr[g, slot] = p.astype(BF16)
        return m_new, l_new, alpha

    def pair(j, states):
        for s in range(per):
            states = tuple(sub_step(g, j * per + s, s % 2, states[g]) for g in range(hps))
        return states

    for g in range(hps):
        t_scr[g, 0] = scores(g, 0)
        p_scr[g, 1] = jnp.zeros((vt_tile, tile), BF16)
        acc_scr[g] = jnp.zeros((FOX_DH, tile), F32)
    init = (jnp.full((1, tile), -jnp.inf, F32), jnp.zeros((1, tile), F32), jnp.ones((1, tile), F32))
    states = lax.fori_loop(0, i, pair, (init,) * hps)
    for s in range(per):
        states = tuple(sub_step(g, i * per + s, s % 2, states[g], diag_part=s, last=(s == per - 1))
                       for g in range(hps))
    last_u = i * per + per - 1
    for g in range(hps):
        _, l_fin, alpha_prev = states[g]
        acc = alpha_prev * acc_scr[g] + jnp.dot(vt_ref[0, g, last_u], p_scr[g, (per - 1) % 2],
                                         preferred_element_type=F32)
        o_ref[:, g * FOX_DH:(g + 1) * FOX_DH] = (acc / l_fin).T.astype(o_ref.dtype)


def _fox_prompt(fq, fkb, fvt, fcum, frow, *, batch, seq, tile, heads_per_step):
    nq = seq // tile
    hps = heads_per_step
    frow3 = frow.reshape(batch * FOX_HEADS, nq, tile)
    nt, vt_tile = fvt.shape[2], fvt.shape[4]
    return pl.pallas_call(
        functools.partial(_fox_prompt_kernel, tile=tile, scale=FOX_DH ** -0.5),
        grid=(batch, FOX_HEADS // hps, nq),
        in_specs=[pl.BlockSpec((tile, hps * FOX_DH), lambda b, h, i: (b * nq + i, h)),
                  pl.BlockSpec((seq, hps * FOX_DH), lambda b, h, i: (b, h)),
                  pl.BlockSpec((1, hps, nt, FOX_DH, vt_tile), lambda b, h, i: (b, h, 0, 0, 0)),
                  pl.BlockSpec((seq, LANES), lambda b, h, i: (b, 0)),
                  pl.BlockSpec((hps, nq, tile), lambda b, h, i: (b * (FOX_HEADS // hps) + h, 0, 0))],
        out_specs=pl.BlockSpec((tile, hps * FOX_DH), lambda b, h, i: (b * nq + i, h)),
        out_shape=jax.ShapeDtypeStruct((batch * seq, FOX_W), BF16),
        scratch_shapes=[pltpu.VMEM((hps, seq, LANES), F32),
                        pltpu.VMEM((hps, 2, vt_tile, tile), F32),
                        pltpu.VMEM((hps, 2, vt_tile, tile), BF16),
                        pltpu.VMEM((hps, FOX_DH, tile), F32)],
        compiler_params=pltpu.CompilerParams(
            dimension_semantics=("parallel", "parallel", "arbitrary")),
        name="fox_prompt",
    )(fq, fkb, fvt, fcum, frow3)


def _fox_sample_kernel(pt_ref, q_ref, kn_ref, vn_ref, ln_ref, *refs, n_pages, scale, page_cols):
    del pt_ref
    kc_refs = refs[:n_pages]
    vc_refs = refs[n_pages:2 * n_pages]
    lc_refs = refs[2 * n_pages:3 * n_pages]
    o_ref, s_scr, g_scr = refs[3 * n_pages:]
    t_new = q_ref.shape[0]
    assert t_new == SUBLANES
    n_rows = FOX_HEADS * t_new
    tshift = t_new.bit_length() - 1
    hmask = FOX_HEADS - 1
    li = lax.broadcasted_iota(jnp.int32, (LANES, LANES), 0)
    lj = lax.broadcasted_iota(jnp.int32, (LANES, LANES), 1)
    same_head = ((li - lj) & hmask) == 0

    def stack_heads(ref):
        return jnp.concatenate([ref[:, h * FOX_DH:(h + 1) * FOX_DH] for h in range(FOX_HEADS)], axis=0)

    prefix = (((li >> tshift) == (lj >> tshift)) & (li <= lj)).astype(F32)
    xn = jnp.broadcast_to(ln_ref[0], (SUBLANES, LANES))
    fn_row = jnp.dot(xn, prefix, precision=HIGHEST, preferred_element_type=F32)[0:1, :]
    eye = (lax.broadcasted_iota(jnp.int32, (n_rows, LANES), 0)
           == lax.broadcasted_iota(jnp.int32, (n_rows, LANES), 1))
    fq = jnp.sum(jnp.where(eye, fn_row, 0.0), axis=1, keepdims=True)

    assert n_pages * FOX_HEADS == LANES and lc_refs[0].shape[1:] == (FOX_HEADS, LANES)
    x = jnp.concatenate([lc_refs[pg][0] for pg in range(n_pages)], axis=0)
    within = jnp.dot(x, (li > lj).astype(F32), precision=HIGHEST, preferred_element_type=F32)
    tot = jnp.broadcast_to(jnp.sum(x, axis=1, keepdims=True), (LANES, LANES))
    later_pages = jnp.dot((same_head & (lj > li)).astype(F32), tot, precision=HIGHEST,
                          preferred_element_type=F32)
    g2 = (within + later_pages) * (-LOG2E)
    hshift = FOX_HEADS.bit_length() - 1
    spread = ((lax.broadcasted_iota(jnp.int32, (LANES, page_cols), 1) >> hshift)
              == lax.broadcasted_iota(jnp.int32, (LANES, page_cols), 0)).astype(BF16)
    g_hi = g2.astype(BF16)
    rest = g2 - g_hi.astype(F32)
    g_mid = rest.astype(BF16)
    g_lo = (rest - g_mid.astype(F32)).astype(BF16)
    g_scr[...] = (jnp.dot(g_hi, spread, preferred_element_type=F32)
                  + jnp.dot(g_mid, spread, preferred_element_type=F32)
                  + jnp.dot(g_lo, spread, preferred_element_type=F32))

    def page_bias(pg):
        return jnp.concatenate(
            [jnp.broadcast_to(g_scr[pg * FOX_HEADS + h:pg * FOX_HEADS + h + 1, :], (t_new, page_cols))
             for h in range(FOX_HEADS)], axis=0)

    c_qk = scale * LOG2E
    fq2 = fq * LOG2E
    qb = stack_heads(q_ref).astype(BF16)
    ri = lax.broadcasted_iota(jnp.int32, (n_rows, page_cols), 0)
    ci = lax.broadcasted_iota(jnp.int32, (n_rows, page_cols), 1)
    head_ok = (ri >> tshift) == (ci & hmask)
    mx = jnp.full((n_rows, 1), -jnp.inf, F32)
    for pg in range(n_pages):
        s = lax.dot_general(qb, kc_refs[pg][0].astype(BF16), NT_DIMS, preferred_element_type=F32)
        s = jnp.where(head_ok, s * c_qk - page_bias(pg), -jnp.inf)
        s_scr[pg] = s
        mx = jnp.maximum(mx, jnp.max(s, axis=1, keepdims=True))

    zpad = jnp.zeros((LANES - n_rows, FOX_DH), BF16)
    kn = jnp.concatenate([stack_heads(kn_ref).astype(BF16), zpad], axis=0)
    vn = jnp.concatenate([stack_heads(vn_ref).astype(BF16), zpad], axis=0)
    r2 = lax.broadcasted_iota(jnp.int32, (n_rows, LANES), 0)
    c2 = lax.broadcasted_iota(jnp.int32, (n_rows, LANES), 1)
    ok = ((r2 >> tshift) == (c2 >> tshift)) & (c2 <= r2)
    s2 = lax.dot_general(qb, kn, NT_DIMS, preferred_element_type=F32)
    s2 = jnp.where(ok, s2 * c_qk - fn_row * LOG2E, -jnp.inf)
    mx = jnp.maximum(mx, jnp.max(s2, axis=1, keepdims=True))

    shift = fq2 - (mx + fq2)
    l = jnp.zeros((n_rows, 1), F32)
    acc = jnp.zeros((n_rows, FOX_DH), F32)
    for pg in range(n_pages):
        pr = jnp.exp2(s_scr[pg] + shift)
        l = l + jnp.sum(pr, axis=1, keepdims=True)
        acc = acc + jnp.dot(pr.astype(BF16), vc_refs[pg][0].astype(BF16), preferred_element_type=F32)
    p2 = jnp.exp2(s2 + shift)
    l = l + jnp.sum(p2, axis=1, keepdims=True)
    acc = acc + jnp.dot(p2.astype(BF16), vn, preferred_element_type=F32)
    out = acc / l
    for h in range(FOX_HEADS):
        o_ref[:, h * FOX_DH:(h + 1) * FOX_DH] = out[h * t_new:(h + 1) * t_new, :]


def _page_index(b, pt, *, j):
    return (pt[b, j], 0, 0)


def _fox_sample(page_table, fq, fk, fv, lf_new, cache_k, cache_v, cache_lf, *, t_new):
    dec_batch, n_pages = page_table.shape
    n_phys, page_size = cache_k.shape[0], cache_k.shape[1]
    n_rows = t_new * FOX_HEADS
    page_cols = page_size * FOX_HEADS
    ln = lf_new.reshape(dec_batch, t_new, FOX_HEADS).transpose(0, 2, 1).reshape(dec_batch, 1, n_rows)
    ln = jnp.pad(ln, ((0, 0), (0, 0), (0, LANES - n_rows)))
    kc = cache_k.reshape(n_phys, page_cols, FOX_DH)
    vc = cache_v.reshape(n_phys, page_cols, FOX_DH)
    lc = jnp.swapaxes(cache_lf, 1, 2)
    new_spec = pl.BlockSpec((t_new, FOX_W), lambda b, pt: (b, 0))
    kv_specs = [pl.BlockSpec((1, page_cols, FOX_DH), functools.partial(_page_index, j=j))
                for j in range(n_pages)]
    lf_specs = [pl.BlockSpec((1, FOX_HEADS, page_size), functools.partial(_page_index, j=j))
                for j in range(n_pages)]
    grid_spec = pltpu.PrefetchScalarGridSpec(
        num_scalar_prefetch=1,
        grid=(dec_batch,),
        in_specs=[new_spec, new_spec, new_spec, pl.BlockSpec((1, 1, LANES), lambda b, pt: (b, 0, 0))]
        + kv_specs + kv_specs + lf_specs,
        out_specs=new_spec,
        scratch_shapes=[pltpu.VMEM((n_pages, n_rows, page_cols), F32),
                        pltpu.VMEM((n_pages * FOX_HEADS, page_cols), F32)],
    )
    return pl.pallas_call(
        functools.partial(_fox_sample_kernel, n_pages=n_pages, scale=FOX_DH ** -0.5,
                          page_cols=page_cols),
        grid_spec=grid_spec,
        out_shape=jax.ShapeDtypeStruct((dec_batch * t_new, FOX_W), F32),
        compiler_params=pltpu.CompilerParams(
            dimension_semantics=("parallel",), vmem_limit_bytes=VMEM_LIMIT_LARGE),
        name="fox_sample",
    )(page_table, fq, fk, fv, ln, *([kc] * n_pages), *([vc] * n_pages), *([lc] * n_pages))


def _merge_kernel(ro_ref, fo_ref, ga_ref, gb_ref, x_ref, wr_ref, wf_ref, wo_ref, g_ref, b_ref, h_ref):
    a = jnp.dot(ro_ref[...].astype(BF16), wr_ref[...], preferred_element_type=F32)
    b = jnp.dot(fo_ref[...].astype(BF16), wf_ref[...], preferred_element_type=F32)
    merged = jax.nn.sigmoid(ga_ref[...]) * a + jax.nn.sigmoid(gb_ref[...]) * b
    mix = jnp.dot(merged.astype(BF16), wo_ref[...], preferred_element_type=F32)
    h_ref[...] = _layer_norm(ALPHA * x_ref[...] + mix, g_ref[...], b_ref[...])


def _merge(ret_o, fox_o, ga, gb, x, wr, wf, wo, g, b, *, tm):
    m = x.shape[0]
    row = lambda w: pl.BlockSpec((tm, w), lambda i: (i, 0))
    return pl.pallas_call(
        _merge_kernel,
        grid=(m // tm,),
        in_specs=[row(RET_V_W), row(FOX_W), row(D_MODEL), row(D_MODEL), row(D_MODEL),
                  _resident(wr.shape), _resident(wf.shape), _resident(wo.shape),
                  _resident(g.shape), _resident(b.shape)],
        out_specs=row(D_MODEL),
        out_shape=jax.ShapeDtypeStruct((m, D_MODEL), F32),
        compiler_params=pltpu.CompilerParams(
            dimension_semantics=("parallel",), vmem_limit_bytes=VMEM_LIMIT_MEDIUM),
        name="merge_out_ln",
    )(ret_o, fox_o, ga, gb, x, wr, wf, wo, g, b)


def _ffn_kernel(h_ref, wu_ref, wd_ref, g_ref, b_ref, y_ref, *, ff_tile):
    h = h_ref[...]
    hb = h.astype(BF16)
    acc = jnp.zeros(h.shape, F32)
    for c in range(D_FF // ff_tile):
        sl = slice(c * ff_tile, (c + 1) * ff_tile)
        u = jnp.maximum(jnp.dot(hb, wu_ref[:, sl], preferred_element_type=F32), 0.0)
        acc = acc + jnp.dot((u * u).astype(BF16), wd_ref[sl, :], preferred_element_type=F32)
    y_ref[...] = _layer_norm(ALPHA * h + acc, g_ref[...], b_ref[...])


def _ffn(h, wu, wd, g, b, *, tm, ff_tile):
    m = h.shape[0]
    row = pl.BlockSpec((tm, D_MODEL), lambda i: (i, 0))
    return pl.pallas_call(
        functools.partial(_ffn_kernel, ff_tile=ff_tile),
        grid=(m // tm,),
        in_specs=[row, _resident(wu.shape), _resident(wd.shape), _resident(g.shape),
                  _resident(b.shape)],
        out_specs=row,
        out_shape=jax.ShapeDtypeStruct((m, D_MODEL), F32),
        compiler_params=pltpu.CompilerParams(
            dimension_semantics=("parallel",), vmem_limit_bytes=VMEM_LIMIT_MEDIUM),
        name="ffn_ln",
    )(h, wu, wd, g, b)


def _rope_tables(pos):
    half = RET_DK // 2
    inv_freq = ROPE_BASE ** (-np.arange(half, dtype=np.float64) / half)
    ang = np.asarray(pos, np.float64)[:, None] * inv_freq[None, :]
    c, s = np.cos(ang), np.sin(ang)
    return (jnp.asarray(np.concatenate([c, c], axis=1), F32),
            jnp.asarray(np.concatenate([-s, s], axis=1), F32))


def kernel(x_prompt, x_sample, cache_k, cache_v, cache_logf, state_ret, page_table, w_in, b_forget,
           ret_gn_gain, w_ret_proj, w_fox_proj, w_out, ln1_g, ln1_b, w_ff_up, w_ff_down, ln2_g,
           ln2_b):
    batch, seq, _ = x_prompt.shape
    dec_batch, t_new, _ = x_sample.shape
    page_size = cache_k.shape[2]
    past_len = page_table.shape[1] * page_size
    tm = ROW_TILE
    assert seq % ATTN_QUERY_TILE == 0 and seq % (CHUNK * RET_CHUNKS_PER_STEP) == 0
    assert (batch * seq) % TAIL_ROW_TILE == 0 and (dec_batch * t_new) % TAIL_ROW_TILE == 0
    assert dec_batch % RET_SAMPLE_GROUP == 0 and tm % t_new == 0

    wt = jnp.swapaxes(w_in[0], 0, 1)
    w_pack = (wt[:C_FF].astype(BF16), wt[C_GATES:].astype(BF16),
              jnp.pad(wt[C_FF:C_GATES], ((0, LANES - FOX_HEADS), (0, 0))).astype(BF16))
    bf_pad = jnp.pad(b_forget[0].astype(F32), (0, LANES - FOX_HEADS)).reshape(1, LANES)
    gain = ret_gn_gain[0].reshape(1, RET_V_W)
    wr, wf, wo = w_ret_proj[0].astype(BF16), w_fox_proj[0].astype(BF16), w_out[0].astype(BF16)
    wu, wd = w_ff_up[0].astype(BF16), w_ff_down[0].astype(BF16)
    g1, b1 = ln1_g[0].reshape(1, D_MODEL), ln1_b[0].reshape(1, D_MODEL)
    g2, b2 = ln2_g[0].reshape(1, D_MODEL), ln2_b[0].reshape(1, D_MODEL)

    def tail(ret_o, fox_o, ga, gb, x2):
        h = _merge(ret_o, fox_o, ga, gb, x2, wr, wf, wo, g1, b1, tm=TAIL_ROW_TILE)
        return _ffn(h, wu, wd, g2, b2, tm=TAIL_ROW_TILE, ff_tile=D_FF)

    xp = x_prompt.reshape(batch * seq, D_MODEL)
    cos_p, sin_p = _rope_tables(np.arange(seq))
    (rq, rk, rv, rg, fq, fk, fv, ga, gb, lf8, lfp, fkb, fvt) = _inproj(
        xp, w_pack, cos_p, sin_p, bf_pad, tm=tm, act_dtype=BF16, prompt_seq=seq)
    fcum, frow = _cumsum(lfp, batch=batch, seq=seq)
    ret_o, s_prompt = _ret_prompt(rq, rk, rv, rg, gain, batch=batch, seq=seq,
                                  chunks_per_step=RET_CHUNKS_PER_STEP)
    fox_o = _fox_prompt(fq, fkb, fvt, fcum, frow, batch=batch, seq=seq, tile=ATTN_QUERY_TILE,
                        heads_per_step=ATTN_HEADS_PER_STEP)
    y_prompt = tail(ret_o, fox_o, ga, gb, xp).reshape(batch, seq, D_MODEL)

    xs = x_sample.reshape(dec_batch * t_new, D_MODEL)
    cos_s, sin_s = _rope_tables(past_len + np.arange(tm) % t_new)
    (rq_s, rk_s, rv_s, rg_s, fq_s, fk_s, fv_s, ga_s, gb_s, lf8_s, _) = _inproj(
        xs, w_pack, cos_s, sin_s, bf_pad, tm=tm, act_dtype=F32)
    ret_o_s, s_sample = _ret_sample(rq_s, rk_s, rv_s, rg_s, gain, state_ret[0], t_new=t_new,
                                    group=RET_SAMPLE_GROUP)
    fox_o_s = _fox_sample(page_table, fq_s, fk_s, fv_s, lf8_s, cache_k[0], cache_v[0], cache_logf[0],
                          t_new=t_new)
    y_sample = tail(ret_o_s, fox_o_s, ga_s, gb_s, xs).reshape(dec_batch, t_new, D_MODEL)

    return (y_prompt, y_sample,
            fk.reshape(1, batch, seq, FOX_HEADS, FOX_DH), fv.reshape(1, batch, seq, FOX_HEADS, FOX_DH),
            lf8.reshape(1, batch, seq, FOX_HEADS), s_prompt[None],
            fk_s.reshape(1, dec_batch, t_new, FOX_HEADS, FOX_DH),
            fv_s.reshape(1, dec_batch, t_new, FOX_HEADS, FOX_DH),
            lf8_s.reshape(1, dec_batch, t_new, FOX_HEADS), s_sample[None])
```

```python
import functools
import math

import jax
import jax.numpy as jnp
import numpy as np
from jax import lax
from jax.experimental import pallas as pl
from jax.experimental.pallas import tpu as pltpu

F32 = jnp.float32
BF16 = jnp.bfloat16
HIGHEST = lax.Precision.HIGHEST

D_MODEL = 1024
RET_HEADS = 4
RET_DK = 128
RET_DV = 256
FOX_HEADS = 8
FOX_DH = 128
D_FF = 4 * D_MODEL
CHUNK = 128
ROPE_BASE = 10000.0
LN_EPS = 1e-5
GN_EPS = 1e-6
DEPTH = 1
ALPHA = (2.0 * DEPTH) ** 0.25
RET_QK_W = RET_HEADS * RET_DK
RET_V_W = RET_HEADS * RET_DV
FOX_W = FOX_HEADS * FOX_DH
LOG2E = math.log2(math.e)
LANES = 128
SUBLANES = 8
VMEM_V7X_BYTES = 64 * 1024 * 1024
VMEM_LIMIT_LARGE = VMEM_V7X_BYTES * 7 // 8
VMEM_LIMIT_MEDIUM = VMEM_V7X_BYTES * 3 // 4

ROW_TILE = 256
TAIL_ROW_TILE = 512
ATTN_QUERY_TILE = 512
ATTN_HEADS_PER_STEP = 4
RET_CHUNKS_PER_STEP = 8
RET_SAMPLE_GROUP = 16

NT_DIMS = (((1,), (1,)), ((), ()))
TN_DIMS = (((0,), (0,)), ((), ()))

C_RQ = 0
C_RK = C_RQ + RET_QK_W
C_RV = C_RK + RET_QK_W
C_RG = C_RV + RET_V_W
C_FQ = C_RG + RET_V_W
C_FK = C_FQ + FOX_W
C_FV = C_FK + FOX_W
C_FF = C_FV + FOX_W
C_GATES = C_FF + FOX_HEADS


def _resident(shape):
    return pl.BlockSpec(shape, lambda *_: (0,) * len(shape), pipeline_mode=pl.Buffered(1))


def _log_sigmoid(z):
    return jnp.minimum(z, 0.0) - jnp.log1p(jnp.exp(-jnp.abs(z)))


def _layer_norm(z, g, b):
    mu = jnp.mean(z, axis=-1, keepdims=True)
    zc = z - mu
    var = jnp.mean(zc * zc, axis=-1, keepdims=True)
    return zc * lax.rsqrt(var + LN_EPS) * g + b


def _ret_log_decay(h):
    return math.log(1.0 - 2.0 ** (-5.0 - h))


def _inproj_kernel(x_ref, w_ref, wg_ref, wf_ref, cos_ref, sin_ref, bf_ref,
                   rq_ref, rk_ref, rv_ref, rg_ref, fq_ref, fk_ref, fv_ref,
                   ga_ref, gb_ref, lf8_ref, lfp_ref, *extra_refs):
    xb = x_ref[...].astype(BF16)

    def mm(lo, hi, ref=w_ref):
        return lax.dot_general(xb, ref[lo:hi, :], NT_DIMS, preferred_element_type=F32)

    cos = cos_ref[...]
    sin = sin_ref[...]

    def rope(v):
        outs = []
        for h in range(RET_HEADS):
            vh = v[:, h * RET_DK:(h + 1) * RET_DK]
            outs.append(vh * cos + pltpu.roll(vh, RET_DK // 2, 1) * sin)
        return jnp.concatenate(outs, axis=1)

    rq_ref[...] = rope(mm(C_RQ, C_RK)).astype(rq_ref.dtype)
    rk_ref[...] = rope(mm(C_RK, C_RV)) * (RET_DK ** -0.5)
    rv_ref[...] = mm(C_RV, C_RG).astype(rv_ref.dtype)
    rg_ref[...] = mm(C_RG, C_FQ)
    fq_ref[...] = mm(C_FQ, C_FK).astype(fq_ref.dtype)
    fk = mm(C_FK, C_FV)
    fk_ref[...] = fk
    fv = mm(C_FV, C_FF)
    fv_ref[...] = fv
    if extra_refs:
        fkb_ref, fvt_ref = extra_refs
        fkb_ref[...] = fk.astype(BF16)
        for h in range(FOX_HEADS):
            fvt_ref[0, h, 0] = fv[:, h * FOX_DH:(h + 1) * FOX_DH].T.astype(BF16)
    ga_ref[...] = mm(0, D_MODEL, wg_ref)
    gb_ref[...] = mm(D_MODEL, 2 * D_MODEL, wg_ref)
    lf = _log_sigmoid(mm(0, LANES, wf_ref) + bf_ref[...])
    lfp_ref[...] = lf
    lf8_ref[...] = lf[:, :FOX_HEADS]


def _inproj(x, weights, cos_t, sin_t, bf_pad, *, tm, act_dtype, prompt_seq=None):
    w_main, w_gates, w_ff = weights
    m = x.shape[0]
    n_tab = cos_t.shape[0] // tm
    row = lambda w: pl.BlockSpec((tm, w), lambda i: (i, 0))
    tab = pl.BlockSpec((tm, LANES), lambda i: (i % n_tab, 0))
    out_shape = [
        jax.ShapeDtypeStruct((m, RET_QK_W), act_dtype),
        jax.ShapeDtypeStruct((m, RET_QK_W), F32),
        jax.ShapeDtypeStruct((m, RET_V_W), act_dtype),
        jax.ShapeDtypeStruct((m, RET_V_W), F32),
        jax.ShapeDtypeStruct((m, FOX_W), act_dtype),
        jax.ShapeDtypeStruct((m, FOX_W), F32),
        jax.ShapeDtypeStruct((m, FOX_W), F32),
        jax.ShapeDtypeStruct((m, D_MODEL), F32),
        jax.ShapeDtypeStruct((m, D_MODEL), F32),
        jax.ShapeDtypeStruct((m, FOX_HEADS), F32),
        jax.ShapeDtypeStruct((m, LANES), F32),
    ]
    out_specs = [row(RET_QK_W), row(RET_QK_W), row(RET_V_W), row(RET_V_W), row(FOX_W),
                 row(FOX_W), row(FOX_W), row(D_MODEL), row(D_MODEL), row(FOX_HEADS), row(LANES)]
    if prompt_seq is not None:
        nt = prompt_seq // tm
        out_shape += [
            jax.ShapeDtypeStruct((m, FOX_W), BF16),
            jax.ShapeDtypeStruct((m // prompt_seq, FOX_HEADS, nt, FOX_DH, tm), BF16),
        ]
        out_specs += [row(FOX_W),
                      pl.BlockSpec((1, FOX_HEADS, 1, FOX_DH, tm), lambda i: (i // nt, 0, i % nt, 0, 0))]
    return pl.pallas_call(
        _inproj_kernel,
        grid=(m // tm,),
        in_specs=[row(D_MODEL), _resident(w_main.shape), _resident(w_gates.shape),
                  _resident(w_ff.shape), tab, tab, _resident(bf_pad.shape)],
        out_specs=out_specs,
        out_shape=out_shape,
        compiler_params=pltpu.CompilerParams(
            dimension_semantics=("parallel",), vmem_limit_bytes=VMEM_LIMIT_LARGE),
        name="inproj",
    )(x, w_main, w_gates, w_ff, cos_t, sin_t, bf_pad)


def _cumsum_kernel(lf_ref, fcol_ref, frow_ref, *, blk):
    t = lf_ref.shape[0]
    ri = lax.broadcasted_iota(jnp.int32, (blk, blk), 0)
    ci = lax.broadcasted_iota(jnp.int32, (blk, blk), 1)
    tri = (ri >= ci).astype(F32)
    carry = jnp.zeros((1, LANES), F32)
    for b in range(t // blk):
        xb = lf_ref[b * blk:(b + 1) * blk, :]
        fb = jnp.dot(tri, xb, precision=HIGHEST, preferred_element_type=F32) + carry
        fcol_ref[b * blk:(b + 1) * blk, :] = fb
        frow_ref[:, b * blk:(b + 1) * blk] = fb.T[:FOX_HEADS, :]
        carry = fb[blk - 1:blk, :]


def _cumsum(lf_pad, *, batch, seq):
    return pl.pallas_call(
        functools.partial(_cumsum_kernel, blk=LANES),
        grid=(batch,),
        in_specs=[pl.BlockSpec((seq, LANES), lambda b: (b, 0))],
        out_specs=(pl.BlockSpec((seq, LANES), lambda b: (b, 0)),
                   pl.BlockSpec((FOX_HEADS, seq), lambda b: (b, 0))),
        out_shape=(jax.ShapeDtypeStruct((batch * seq, LANES), F32),
                   jax.ShapeDtypeStruct((batch * FOX_HEADS, seq), F32)),
        compiler_params=pltpu.CompilerParams(dimension_semantics=("parallel",)),
        name="logf_cumsum",
    )(lf_pad)


def _ret_head_out(o, gain, rg):
    mu = jnp.mean(o, axis=-1, keepdims=True)
    oc = o - mu
    var = jnp.mean(oc * oc, axis=-1, keepdims=True)
    y = oc * lax.rsqrt(var + GN_EPS) * gain
    return y * (rg * jax.nn.sigmoid(rg))


def _ret_prompt_kernel(q_ref, k_ref, v_ref, rg_ref, gain_ref, o_ref, sfin_ref, s_scr, *, chunk):
    c = pl.program_id(1)

    @pl.when(c == 0)
    def _():
        s_scr[...] = jnp.zeros_like(s_scr)

    ii = lax.broadcasted_iota(jnp.int32, (chunk, chunk), 0)
    jj = lax.broadcasted_iota(jnp.int32, (chunk, chunk), 1)
    causal = ii >= jj
    expo = jnp.where(causal, (ii - jj).astype(F32), 0.0)
    ic = lax.broadcasted_iota(jnp.int32, (chunk, 1), 0).astype(F32)
    for h in range(RET_HEADS):
        ld = _ret_log_decay(h)
        dmat = jnp.where(causal, jnp.exp(expo * ld), 0.0)
        cross_decay = jnp.exp((ic + 1.0) * ld)
        k_decay = jnp.exp((chunk - 1.0 - ic) * ld)
        chunk_decay = math.exp(chunk * ld)
        s_run = s_scr[h]
        for cc in range(q_ref.shape[0] // chunk):
            rows = slice(cc * chunk, (cc + 1) * chunk)
            qh = q_ref[rows, h * RET_DK:(h + 1) * RET_DK].astype(BF16)
            kh = k_ref[rows, h * RET_DK:(h + 1) * RET_DK]
            vh = v_ref[rows, h * RET_DV:(h + 1) * RET_DV].astype(BF16)
            sc = lax.dot_general(qh, kh.astype(BF16), NT_DIMS, preferred_element_type=F32) * dmat
            intra = jnp.dot(sc.astype(BF16), vh, preferred_element_type=F32)
            cross = jnp.dot(qh, s_run.astype(BF16), preferred_element_type=F32) * cross_decay
            kd = (kh * k_decay).astype(BF16)
            s_run = s_run * chunk_decay + lax.dot_general(kd, vh, TN_DIMS, preferred_element_type=F32)
            sl = slice(h * RET_DV, (h + 1) * RET_DV)
            o_ref[rows, sl] = _ret_head_out(
                intra + cross, gain_ref[:, sl], rg_ref[rows, sl]).astype(o_ref.dtype)
        s_scr[h] = s_run

    @pl.when(c == pl.num_programs(1) - 1)
    def _():
        sfin_ref[0] = s_scr[...]


def _ret_prompt(rq, rk, rv, rg, gain, *, batch, seq, chunks_per_step):
    rows = CHUNK * chunks_per_step
    nc = seq // rows
    row = lambda w: pl.BlockSpec((rows, w), lambda b, c: (b * nc + c, 0))
    return pl.pallas_call(
        functools.partial(_ret_prompt_kernel, chunk=CHUNK),
        grid=(batch, nc),
        in_specs=[row(RET_QK_W), row(RET_QK_W), row(RET_V_W), row(RET_V_W),
                  pl.BlockSpec((1, RET_V_W), lambda b, c: (0, 0))],
        out_specs=(row(RET_V_W),
                   pl.BlockSpec((1, RET_HEADS, RET_DK, RET_DV), lambda b, c: (b, 0, 0, 0))),
        out_shape=(jax.ShapeDtypeStruct((batch * seq, RET_V_W), BF16),
                   jax.ShapeDtypeStruct((batch, RET_HEADS, RET_DK, RET_DV), F32)),
        scratch_shapes=[pltpu.VMEM((RET_HEADS, RET_DK, RET_DV), F32)],
        compiler_params=pltpu.CompilerParams(
            dimension_semantics=("parallel", "arbitrary"), vmem_limit_bytes=VMEM_LIMIT_MEDIUM),
        name="ret_prompt",
    )(rq, rk, rv, rg, gain)


def _ret_sample_kernel(q_ref, k_ref, v_ref, rg_ref, gain_ref, s0_ref, o_ref, s1_ref, *, group, t_new):
    rows = group * t_new
    shift = t_new.bit_length() - 1
    ri = lax.broadcasted_iota(jnp.int32, (rows, rows), 0)
    ci = lax.broadcasted_iota(jnp.int32, (rows, rows), 1)
    valid = ((ri >> shift) == (ci >> shift)) & (ri >= ci)
    expo = jnp.where(valid, (ri - ci).astype(F32), 0.0)
    r1 = lax.broadcasted_iota(jnp.int32, (rows, 1), 0)
    tpos = (r1 & (t_new - 1)).astype(F32)
    rowgrp = lax.broadcasted_iota(jnp.int32, (rows, RET_DK), 0) >> shift
    for h in range(RET_HEADS):
        ld = _ret_log_decay(h)
        dmat = jnp.where(valid, jnp.exp(expo * ld), 0.0)
        cross_decay = jnp.exp((tpos + 1.0) * ld)
        k_decay = jnp.exp((t_new - 1.0 - tpos) * ld)
        chunk_decay = math.exp(t_new * ld)
        qh = q_ref[:, h * RET_DK:(h + 1) * RET_DK]
        kh = k_ref[:, h * RET_DK:(h + 1) * RET_DK]
        vh = v_ref[:, h * RET_DV:(h + 1) * RET_DV].astype(BF16)
        qb = qh.astype(BF16)
        sc = lax.dot_general(qb, kh.astype(BF16), NT_DIMS, preferred_element_type=F32) * dmat
        o = jnp.dot(sc.astype(BF16), vh, preferred_element_type=F32)
        kd = kh * k_decay
        cross = jnp.zeros((rows, RET_DV), F32)
        for g in range(group):
            s_old = s0_ref[g, h]
            in_g = rowgrp == g
            qg = jnp.where(in_g, qh, 0.0).astype(BF16)
            cross = cross + jnp.dot(qg, s_old.astype(BF16), preferred_element_type=F32)
            kg = jnp.where(in_g, kd, 0.0).astype(BF16)
            s1_ref[g, h] = s_old * chunk_decay + lax.dot_general(
                kg, vh, TN_DIMS, preferred_element_type=F32)
        o = o + cross * cross_decay
        sl = slice(h * RET_DV, (h + 1) * RET_DV)
        o_ref[:, sl] = _ret_head_out(o, gain_ref[:, sl], rg_ref[:, sl]).astype(o_ref.dtype)


def _ret_sample(rq, rk, rv, rg, gain, state, *, t_new, group):
    m = rq.shape[0]
    rows = group * t_new
    row = lambda w: pl.BlockSpec((rows, w), lambda i: (i, 0))
    st = pl.BlockSpec((group, RET_HEADS, RET_DK, RET_DV), lambda i: (i, 0, 0, 0))
    return pl.pallas_call(
        functools.partial(_ret_sample_kernel, group=group, t_new=t_new),
        grid=(m // rows,),
        in_specs=[row(RET_QK_W), row(RET_QK_W), row(RET_V_W), row(RET_V_W),
                  pl.BlockSpec((1, RET_V_W), lambda i: (0, 0)), st],
        out_specs=(row(RET_V_W), st),
        out_shape=(jax.ShapeDtypeStruct((m, RET_V_W), BF16),
                   jax.ShapeDtypeStruct(state.shape, F32)),
        compiler_params=pltpu.CompilerParams(
            dimension_semantics=("parallel",), vmem_limit_bytes=VMEM_LIMIT_MEDIUM),
        name="ret_sample",
    )(rq, rk, rv, rg, gain, state)


def _fox_prompt_kernel(q_ref, k_ref, vt_ref, fcum_ref, frow_ref, o_ref,
                       fk_scr, t_scr, p_scr, acc_scr, *, tile, scale):
    hps = vt_ref.shape[1]
    h0 = pl.program_id(1) * hps
    i = pl.program_id(2)
    seq = k_ref.shape[0]
    vt_tile = vt_ref.shape[-1]
    per = tile // vt_tile
    assert per % 2 == 0

    @pl.when(i == 0)
    def _():
        lane = lax.broadcasted_iota(jnp.int32, (seq, LANES), 1)
        for g in range(hps):
            col = jnp.sum(jnp.where(lane == h0 + g, fcum_ref[...], 0.0), axis=1, keepdims=True)
            fk_scr[g] = jnp.broadcast_to(col * LOG2E, (seq, LANES))

    def scores(g, u):
        start = pl.multiple_of(u * vt_tile, vt_tile)
        ks = k_ref[pl.ds(start, vt_tile), g * FOX_DH:(g + 1) * FOX_DH]
        fk2 = fk_scr[g, pl.ds(start, vt_tile), :]
        t = lax.dot_general(ks, q_ref[:, g * FOX_DH:(g + 1) * FOX_DH], NT_DIMS,
                            preferred_element_type=F32) * (scale * LOG2E)
        return t - jnp.concatenate([fk2] * (tile // LANES), axis=1)

    def sub_step(g, u, slot, state, *, diag_part=None, last=False):
        m_run, l_run, alpha_prev = state
        fq2 = frow_ref[g, pl.ds(i, 1), :] * LOG2E
        if not last:
            t_scr[g, 1 - slot] = scores(g, u + 1)
        pv_prev = jnp.dot(vt_ref[0, g, jnp.maximum(u - 1, 0)], p_scr[g, 1 - slot],
                          preferred_element_type=F32)
        acc_scr[g] = alpha_prev * acc_scr[g] + pv_prev
        t = t_scr[g, slot]
        if diag_part is not None:
            key = diag_part * vt_tile + lax.broadcasted_iota(jnp.int32, (vt_tile, tile), 0)
            qry = lax.broadcasted_iota(jnp.int32, (vt_tile, tile), 1)
            t = jnp.where(key <= qry, t, -jnp.inf)
        m_new = jnp.maximum(m_run, jnp.max(t, axis=0, keepdims=True) + fq2)
        alpha = jnp.exp2(m_run - m_new)
        p = jnp.exp2(t + (fq2 - m_new))
        l_new = alpha * l_run + jnp.sum(p, axis=0, keepdims=True)
        p_scr[g, slot] = p.astype(BF16)
        return m_new, l_new, alpha

    def pair(j, states):
        for s in range(per):
            states = tuple(sub_step(g, j * per + s, s % 2, states[g]) for g in range(hps))
        return states

    for g in range(hps):
        t_scr[g, 0] = scores(g, 0)
        p_scr[g, 1] = jnp.zeros((vt_tile, tile), BF16)
        acc_scr[g] = jnp.zeros((FOX_DH, tile), F32)
    init = (jnp.full((1, tile), -jnp.inf, F32), jnp.zeros((1, tile), F32), jnp.ones((1, tile), F32))
    states = lax.fori_loop(0, i, pair, (init,) * hps)
    for s in range(per):
        states = tuple(sub_step(g, i * per + s, s % 2, states[g], diag_part=s, last=(s == per - 1))
                       for g in range(hps))
    last_u = i * per + per - 1
    for g in range(hps):
        _, l_fin, alpha_prev = states[g]
        acc = alpha_prev * acc_scr[g] + jnp.dot(vt_ref[0, g, last_u], p_scr[g, (per - 1) % 2],
                                         preferred_element_type=F32)
        o_ref[:, g * FOX_DH:(g + 1) * FOX_DH] = (acc / l_fin).T.astype(o_ref.dtype)


def _fox_prompt(fq, fkb, fvt, fcum, frow, *, batch, seq, tile, heads_per_step):
    nq = seq // tile
    hps = heads_per_step
    frow3 = frow.reshape(batch * FOX_HEADS, nq, tile)
    nt, vt_tile = fvt.shape[2], fvt.shape[4]
    return pl.pallas_call(
        functools.partial(_fox_prompt_kernel, tile=tile, scale=FOX_DH ** -0.5),
        grid=(batch, FOX_HEADS // hps, nq),
        in_specs=[pl.BlockSpec((tile, hps * FOX_DH), lambda b, h, i: (b * nq + i, h)),
                  pl.BlockSpec((seq, hps * FOX_DH), lambda b, h, i: (b, h)),
                  pl.BlockSpec((1, hps, nt, FOX_DH, vt_tile), lambda b, h, i: (b, h, 0, 0, 0)),
                  pl.BlockSpec((seq, LANES), lambda b, h, i: (b, 0)),
                  pl.BlockSpec((hps, nq, tile), lambda b, h, i: (b * (FOX_HEADS // hps) + h, 0, 0))],
        out_specs=pl.BlockSpec((tile, hps * FOX_DH), lambda b, h, i: (b * nq + i, h)),
        out_shape=jax.ShapeDtypeStruct((batch * seq, FOX_W), BF16),
        scratch_shapes=[pltpu.VMEM((hps, seq, LANES), F32),
                        pltpu.VMEM((hps, 2, vt_tile, tile), F32),
                        pltpu.VMEM((hps, 2, vt_tile, tile), BF16),
                        pltpu.VMEM((hps, FOX_DH, tile), F32)],
        compiler_params=pltpu.CompilerParams(
            dimension_semantics=("parallel", "parallel", "arbitrary")),
        name="fox_prompt",
    )(fq, fkb, fvt, fcum, frow3)


def _fox_sample_kernel(pt_ref, q_ref, kn_ref, vn_ref, ln_ref, *refs, n_pages, scale, page_cols):
    del pt_ref
    kc_refs = refs[:n_pages]
    vc_refs = refs[n_pages:2 * n_pages]
    lc_refs = refs[2 * n_pages:3 * n_pages]
    o_ref, s_scr, g_scr = refs[3 * n_pages:]
    t_new = q_ref.shape[0]
    assert t_new == SUBLANES
    n_rows = FOX_HEADS * t_new
    tshift = t_new.bit_length() - 1
    hmask = FOX_HEADS - 1
    li = lax.broadcasted_iota(jnp.int32, (LANES, LANES), 0)
    lj = lax.broadcasted_iota(jnp.int32, (LANES, LANES), 1)
    same_head = ((li - lj) & hmask) == 0

    def stack_heads(ref):
        return jnp.concatenate([ref[:, h * FOX_DH:(h + 1) * FOX_DH] for h in range(FOX_HEADS)], axis=0)

    prefix = (((li >> tshift) == (lj >> tshift)) & (li <= lj)).astype(F32)
    xn = jnp.broadcast_to(ln_ref[0], (SUBLANES, LANES))
    fn_row = jnp.dot(xn, prefix, precision=HIGHEST, preferred_element_type=F32)[0:1, :]
    eye = (lax.broadcasted_iota(jnp.int32, (n_rows, LANES), 0)
           == lax.broadcasted_iota(jnp.int32, (n_rows, LANES), 1))
    fq = jnp.sum(jnp.where(eye, fn_row, 0.0), axis=1, keepdims=True)

    assert n_pages * FOX_HEADS == LANES and lc_refs[0].shape[1:] == (FOX_HEADS, LANES)
    x = jnp.concatenate([lc_refs[pg][0] for pg in range(n_pages)], axis=0)
    within = jnp.dot(x, (li > lj).astype(F32), precision=HIGHEST, preferred_element_type=F32)
    tot = jnp.broadcast_to(jnp.sum(x, axis=1, keepdims=True), (LANES, LANES))
    later_pages = jnp.dot((same_head & (lj > li)).astype(F32), tot, precision=HIGHEST,
                          preferred_element_type=F32)
    g2 = (within + later_pages) * (-LOG2E)
    hshift = FOX_HEADS.bit_length() - 1
    spread = ((lax.broadcasted_iota(jnp.int32, (LANES, page_cols), 1) >> hshift)
              == lax.broadcasted_iota(jnp.int32, (LANES, page_cols), 0)).astype(BF16)
    g_hi = g2.astype(BF16)
    rest = g2 - g_hi.astype(F32)
    g_mid = rest.astype(BF16)
    g_lo = (rest - g_mid.astype(F32)).astype(BF16)
    g_cols = (jnp.dot(g_hi, spread, preferred_element_type=F32)
              + jnp.dot(g_mid, spread, preferred_element_type=F32)
              + jnp.dot(g_lo, spread, preferred_element_type=F32))
    own_head = ((lax.broadcasted_iota(jnp.int32, (LANES, page_cols), 1)
                 - lax.broadcasted_iota(jnp.int32, (LANES, page_cols), 0)) & hmask) == 0
    g_scr[...] = jnp.where(own_head, g_cols, jnp.inf)

    def page_bias(pg):
        return jnp.concatenate(
            [jnp.broadcast_to(g_scr[pg * FOX_HEADS + h:pg * FOX_HEADS + h + 1, :], (t_new, page_cols))
             for h in range(FOX_HEADS)], axis=0)

    c_qk = scale * LOG2E
    fq2 = fq * LOG2E
    qb = stack_heads(q_ref).astype(BF16)
    mx = jnp.full((n_rows, 1), -jnp.inf, F32)
    for pg in range(n_pages):
        s = lax.dot_general(qb, kc_refs[pg][0].astype(BF16), NT_DIMS, preferred_element_type=F32)
        s = s * c_qk - page_bias(pg)
        s_scr[pg] = s
        mx = jnp.maximum(mx, jnp.max(s, axis=1, keepdims=True))

    zpad = jnp.zeros((LANES - n_rows, FOX_DH), BF16)
    kn = jnp.concatenate([stack_heads(kn_ref).astype(BF16), zpad], axis=0)
    vn = jnp.concatenate([stack_heads(vn_ref).astype(BF16), zpad], axis=0)
    r2 = lax.broadcasted_iota(jnp.int32, (n_rows, LANES), 0)
    c2 = lax.broadcasted_iota(jnp.int32, (n_rows, LANES), 1)
    ok = ((r2 >> tshift) == (c2 >> tshift)) & (c2 <= r2)
    s2 = lax.dot_general(qb, kn, NT_DIMS, preferred_element_type=F32)
    s2 = jnp.where(ok, s2 * c_qk - fn_row * LOG2E, -jnp.inf)
    mx = jnp.maximum(mx, jnp.max(s2, axis=1, keepdims=True))

    shift = fq2 - (mx + fq2)
    l = jnp.zeros((n_rows, 1), F32)
    acc = jnp.zeros((n_rows, FOX_DH), F32)
    for pg in range(n_pages):
        pr = jnp.exp2(s_scr[pg] + shift)
        l = l + jnp.sum(pr, axis=1, keepdims=True)
        acc = acc + jnp.dot(pr.astype(BF16), vc_refs[pg][0].astype(BF16), preferred_element_type=F32)
    p2 = jnp.exp2(s2 + shift)
    l = l + jnp.sum(p2, axis=1, keepdims=True)
    acc = acc + jnp.dot(p2.astype(BF16), vn, preferred_element_type=F32)
    out = acc / l
    for h in range(FOX_HEADS):
        o_ref[:, h * FOX_DH:(h + 1) * FOX_DH] = out[h * t_new:(h + 1) * t_new, :]


def _page_index(b, pt, *, j):
    return (pt[b, j], 0, 0)


def _fox_sample(page_table, fq, fk, fv, lf_new, cache_k, cache_v, cache_lf, *, t_new):
    dec_batch, n_pages = page_table.shape
    n_phys, page_size = cache_k.shape[0], cache_k.shape[1]
    n_rows = t_new * FOX_HEADS
    page_cols = page_size * FOX_HEADS
    ln = lf_new.reshape(dec_batch, t_new, FOX_HEADS).transpose(0, 2, 1).reshape(dec_batch, 1, n_rows)
    ln = jnp.pad(ln, ((0, 0), (0, 0), (0, LANES - n_rows)))
    kc = cache_k.reshape(n_phys, page_cols, FOX_DH)
    vc = cache_v.reshape(n_phys, page_cols, FOX_DH)
    lc = jnp.swapaxes(cache_lf, 1, 2)
    new_spec = pl.BlockSpec((t_new, FOX_W), lambda b, pt: (b, 0))
    kv_specs = [pl.BlockSpec((1, page_cols, FOX_DH), functools.partial(_page_index, j=j))
                for j in range(n_pages)]
    lf_specs = [pl.BlockSpec((1, FOX_HEADS, page_size), functools.partial(_page_index, j=j))
                for j in range(n_pages)]
    grid_spec = pltpu.PrefetchScalarGridSpec(
        num_scalar_prefetch=1,
        grid=(dec_batch,),
        in_specs=[new_spec, new_spec, new_spec, pl.BlockSpec((1, 1, LANES), lambda b, pt: (b, 0, 0))]
        + kv_specs + kv_specs + lf_specs,
        out_specs=new_spec,
        scratch_shapes=[pltpu.VMEM((n_pages, n_rows, page_cols), F32),
                        pltpu.VMEM((n_pages * FOX_HEADS, page_cols), F32)],
    )
    return pl.pallas_call(
        functools.partial(_fox_sample_kernel, n_pages=n_pages, scale=FOX_DH ** -0.5,
                          page_cols=page_cols),
        grid_spec=grid_spec,
        out_shape=jax.ShapeDtypeStruct((dec_batch * t_new, FOX_W), F32),
        compiler_params=pltpu.CompilerParams(
            dimension_semantics=("parallel",), vmem_limit_bytes=VMEM_LIMIT_LARGE),
        name="fox_sample",
    )(page_table, fq, fk, fv, ln, *([kc] * n_pages), *([vc] * n_pages), *([lc] * n_pages))


def _merge_kernel(ro_ref, fo_ref, ga_ref, gb_ref, x_ref, wr_ref, wf_ref, wo_ref, g_ref, b_ref, h_ref):
    a = jnp.dot(ro_ref[...].astype(BF16), wr_ref[...], preferred_element_type=F32)
    b = jnp.dot(fo_ref[...].astype(BF16), wf_ref[...], preferred_element_type=F32)
    merged = jax.nn.sigmoid(ga_ref[...]) * a + jax.nn.sigmoid(gb_ref[...]) * b
    mix = jnp.dot(merged.astype(BF16), wo_ref[...], preferred_element_type=F32)
    h_ref[...] = _layer_norm(ALPHA * x_ref[...] + mix, g_ref[...], b_ref[...])


def _merge(ret_o, fox_o, ga, gb, x, wr, wf, wo, g, b, *, tm):
    m = x.shape[0]
    row = lambda w: pl.BlockSpec((tm, w), lambda i: (i, 0))
    return pl.pallas_call(
        _merge_kernel,
        grid=(m // tm,),
        in_specs=[row(RET_V_W), row(FOX_W), row(D_MODEL), row(D_MODEL), row(D_MODEL),
                  _resident(wr.shape), _resident(wf.shape), _resident(wo.shape),
                  _resident(g.shape), _resident(b.shape)],
        out_specs=row(D_MODEL),
        out_shape=jax.ShapeDtypeStruct((m, D_MODEL), F32),
        compiler_params=pltpu.CompilerParams(
            dimension_semantics=("parallel",), vmem_limit_bytes=VMEM_LIMIT_MEDIUM),
        name="merge_out_ln",
    )(ret_o, fox_o, ga, gb, x, wr, wf, wo, g, b)


def _ffn_kernel(h_ref, wu_ref, wd_ref, g_ref, b_ref, y_ref, *, ff_tile):
    h = h_ref[...]
    hb = h.astype(BF16)
    acc = jnp.zeros(h.shape, F32)
    for c in range(D_FF // ff_tile):
        sl = slice(c * ff_tile, (c + 1) * ff_tile)
        u = jnp.maximum(jnp.dot(hb, wu_ref[:, sl], preferred_element_type=F32), 0.0)
        acc = acc + jnp.dot((u * u).astype(BF16), wd_ref[sl, :], preferred_element_type=F32)
    y_ref[...] = _layer_norm(ALPHA * h + acc, g_ref[...], b_ref[...])


def _ffn(h, wu, wd, g, b, *, tm, ff_tile):
    m = h.shape[0]
    row = pl.BlockSpec((tm, D_MODEL), lambda i: (i, 0))
    return pl.pallas_call(
        functools.partial(_ffn_kernel, ff_tile=ff_tile),
        grid=(m // tm,),
        in_specs=[row, _resident(wu.shape), _resident(wd.shape), _resident(g.shape),
                  _resident(b.shape)],
        out_specs=row,
        out_shape=jax.ShapeDtypeStruct((m, D_MODEL), F32),
        compiler_params=pltpu.CompilerParams(
            dimension_semantics=("parallel",), vmem_limit_bytes=VMEM_LIMIT_MEDIUM),
        name="ffn_ln",
    )(h, wu, wd, g, b)


def _rope_tables(pos):
    half = RET_DK // 2
    inv_freq = ROPE_BASE ** (-np.arange(half, dtype=np.float64) / half)
    ang = np.asarray(pos, np.float64)[:, None] * inv_freq[None, :]
    c, s = np.cos(ang), np.sin(ang)
    return (jnp.asarray(np.concatenate([c, c], axis=1), F32),
            jnp.asarray(np.concatenate([-s, s], axis=1), F32))


def kernel(x_prompt, x_sample, cache_k, cache_v, cache_logf, state_ret, page_table, w_in, b_forget,
           ret_gn_gain, w_ret_proj, w_fox_proj, w_out, ln1_g, ln1_b, w_ff_up, w_ff_down, ln2_g,
           ln2_b):
    batch, seq, _ = x_prompt.shape
    dec_batch, t_new, _ = x_sample.shape
    page_size = cache_k.shape[2]
    past_len = page_table.shape[1] * page_size
    tm = ROW_TILE
    assert seq % ATTN_QUERY_TILE == 0 and seq % (CHUNK * RET_CHUNKS_PER_STEP) == 0
    assert (batch * seq) % TAIL_ROW_TILE == 0 and (dec_batch * t_new) % TAIL_ROW_TILE == 0
    assert dec_batch % RET_SAMPLE_GROUP == 0 and tm % t_new == 0

    wt = jnp.swapaxes(w_in[0], 0, 1)
    w_pack = (wt[:C_FF].astype(BF16), wt[C_GATES:].astype(BF16),
              jnp.pad(wt[C_FF:C_GATES], ((0, LANES - FOX_HEADS), (0, 0))).astype(BF16))
    bf_pad = jnp.pad(b_forget[0].astype(F32), (0, LANES - FOX_HEADS)).reshape(1, LANES)
    gain = ret_gn_gain[0].reshape(1, RET_V_W)
    wr, wf, wo = w_ret_proj[0].astype(BF16), w_fox_proj[0].astype(BF16), w_out[0].astype(BF16)
    wu, wd = w_ff_up[0].astype(BF16), w_ff_down[0].astype(BF16)
    g1, b1 = ln1_g[0].reshape(1, D_MODEL), ln1_b[0].reshape(1, D_MODEL)
    g2, b2 = ln2_g[0].reshape(1, D_MODEL), ln2_b[0].reshape(1, D_MODEL)

    def tail(ret_o, fox_o, ga, gb, x2):
        h = _merge(ret_o, fox_o, ga, gb, x2, wr, wf, wo, g1, b1, tm=TAIL_ROW_TILE)
        return _ffn(h, wu, wd, g2, b2, tm=TAIL_ROW_TILE, ff_tile=D_FF)

    xp = x_prompt.reshape(batch * seq, D_MODEL)
    cos_p, sin_p = _rope_tables(np.arange(seq))
    (rq, rk, rv, rg, fq, fk, fv, ga, gb, lf8, lfp, fkb, fvt) = _inproj(
        xp, w_pack, cos_p, sin_p, bf_pad, tm=tm, act_dtype=BF16, prompt_seq=seq)
    fcum, frow = _cumsum(lfp, batch=batch, seq=seq)
    fox_o = _fox_prompt(fq, fkb, fvt, fcum, frow, batch=batch, seq=seq, tile=ATTN_QUERY_TILE,
                        heads_per_step=ATTN_HEADS_PER_STEP)
    ret_o, s_prompt = _ret_prompt(rq, rk, rv, rg, gain, batch=batch, seq=seq,
                                  chunks_per_step=RET_CHUNKS_PER_STEP)
    y_prompt = tail(ret_o, fox_o, ga, gb, xp).reshape(batch, seq, D_MODEL)

    xs = x_sample.reshape(dec_batch * t_new, D_MODEL)
    cos_s, sin_s = _rope_tables(past_len + np.arange(tm) % t_new)
    (rq_s, rk_s, rv_s, rg_s, fq_s, fk_s, fv_s, ga_s, gb_s, lf8_s, _) = _inproj(
        xs, w_pack, cos_s, sin_s, bf_pad, tm=tm, act_dtype=F32)
    ret_o_s, s_sample = _ret_sample(rq_s, rk_s, rv_s, rg_s, gain, state_ret[0], t_new=t_new,
                                    group=RET_SAMPLE_GROUP)
    fox_o_s = _fox_sample(page_table, fq_s, fk_s, fv_s, lf8_s, cache_k[0], cache_v[0], cache_logf[0],
                          t_new=t_new)
    y_sample = tail(ret_o_s, fox_o_s, ga_s, gb_s, xs).reshape(dec_batch, t_new, D_MODEL)

    return (y_prompt, y_sample,
            fk.reshape(1, batch, seq, FOX_HEADS, FOX_DH), fv.reshape(1, batch, seq, FOX_HEADS, FOX_DH),
            lf8.reshape(1, batch, seq, FOX_HEADS), s_prompt[None],
            fk_s.reshape(1, dec_batch, t_new, FOX_HEADS, FOX_DH),
            fv_s.reshape(1, dec_batch, t_new, FOX_HEADS, FOX_DH),
            lf8_s.reshape(1, dec_batch, t_new, FOX_HEADS), s_sample[None])
```

```python
import functools
import math

import jax
import jax.numpy as jnp
import numpy as np
from jax import lax
from jax.experimental import pallas as pl
from jax.experimental.pallas import tpu as pltpu

F32 = jnp.float32
BF16 = jnp.bfloat16
HIGHEST = lax.Precision.HIGHEST

D_MODEL = 1024
RET_HEADS = 4
RET_DK = 128
RET_DV = 256
FOX_HEADS = 8
FOX_DH = 128
D_FF = 4 * D_MODEL
CHUNK = 128
ROPE_BASE = 10000.0
LN_EPS = 1e-5
GN_EPS = 1e-6
DEPTH = 1
ALPHA = (2.0 * DEPTH) ** 0.25
RET_QK_W = RET_HEADS * RET_DK
RET_V_W = RET_HEADS * RET_DV
FOX_W = FOX_HEADS * FOX_DH
LOG2E = math.log2(math.e)
LANES = 128
SUBLANES = 8
VMEM_V7X_BYTES = 64 * 1024 * 1024
VMEM_LIMIT_LARGE = VMEM_V7X_BYTES * 7 // 8
VMEM_LIMIT_MEDIUM = VMEM_V7X_BYTES * 3 // 4

ROW_TILE = 256
TAIL_ROW_TILE = 512
ATTN_QUERY_TILE = 512
ATTN_HEADS_PER_STEP = 4
RET_CHUNKS_PER_STEP = 8
RET_SAMPLE_GROUP = 16

NT_DIMS = (((1,), (1,)), ((), ()))
TN_DIMS = (((0,), (0,)), ((), ()))

C_RQ = 0
C_RK = C_RQ + RET_QK_W
C_RV = C_RK + RET_QK_W
C_RG = C_RV + RET_V_W
C_FQ = C_RG + RET_V_W
C_FK = C_FQ + FOX_W
C_FV = C_FK + FOX_W
C_FF = C_FV + FOX_W
C_GATES = C_FF + FOX_HEADS


def _resident(shape):
    return pl.BlockSpec(shape, lambda *_: (0,) * len(shape), pipeline_mode=pl.Buffered(1))


def _log_sigmoid(z):
    return jnp.minimum(z, 0.0) - jnp.log1p(jnp.exp(-jnp.abs(z)))


def _layer_norm(z, g, b):
    mu = jnp.mean(z, axis=-1, keepdims=True)
    zc = z - mu
    var = jnp.mean(zc * zc, axis=-1, keepdims=True)
    return zc * lax.rsqrt(var + LN_EPS) * g + b


def _ret_log_decay(h):
    return math.log(1.0 - 2.0 ** (-5.0 - h))


def _inproj_kernel(x_ref, w_ref, wg_ref, wf_ref, cos_ref, sin_ref, bf_ref,
                   rq_ref, rk_ref, rv_ref, rg_ref, fq_ref, fk_ref, fv_ref,
                   ga_ref, gb_ref, lf8_ref, lfp_ref, *extra_refs):
    xb = x_ref[...].astype(BF16)

    def mm(lo, hi, ref=w_ref):
        return lax.dot_general(xb, ref[lo:hi, :], NT_DIMS, preferred_element_type=F32)

    cos = cos_ref[...]
    sin = sin_ref[...]

    def rope(v):
        outs = []
        for h in range(RET_HEADS):
            vh = v[:, h * RET_DK:(h + 1) * RET_DK]
            outs.append(vh * cos + pltpu.roll(vh, RET_DK // 2, 1) * sin)
        return jnp.concatenate(outs, axis=1)

    rq_ref[...] = rope(mm(C_RQ, C_RK)).astype(rq_ref.dtype)
    rk_ref[...] = rope(mm(C_RK, C_RV)) * (RET_DK ** -0.5)
    rv_ref[...] = mm(C_RV, C_RG).astype(rv_ref.dtype)
    rg_ref[...] = mm(C_RG, C_FQ)
    fq_ref[...] = mm(C_FQ, C_FK).astype(fq_ref.dtype)
    fk = mm(C_FK, C_FV)
    fk_ref[...] = fk
    fv = mm(C_FV, C_FF)
    fv_ref[...] = fv
    if extra_refs:
        fkb_ref, fvt_ref = extra_refs
        fkb_ref[...] = fk.astype(BF16)
        for h in range(FOX_HEADS):
            fvt_ref[0, h, 0] = fv[:, h * FOX_DH:(h + 1) * FOX_DH].T.astype(BF16)
    ga_ref[...] = mm(0, D_MODEL, wg_ref)
    gb_ref[...] = mm(D_MODEL, 2 * D_MODEL, wg_ref)
    lf = _log_sigmoid(mm(0, LANES, wf_ref) + bf_ref[...])
    lfp_ref[...] = lf
    lf8_ref[...] = lf[:, :FOX_HEADS]


def _inproj(x, weights, cos_t, sin_t, bf_pad, *, tm, act_dtype, prompt_seq=None):
    w_main, w_gates, w_ff = weights
    m = x.shape[0]
    n_tab = cos_t.shape[0] // tm
    row = lambda w: pl.BlockSpec((tm, w), lambda i: (i, 0))
    tab = pl.BlockSpec((tm, LANES), lambda i: (i % n_tab, 0))
    out_shape = [
        jax.ShapeDtypeStruct((m, RET_QK_W), act_dtype),
        jax.ShapeDtypeStruct((m, RET_QK_W), F32),
        jax.ShapeDtypeStruct((m, RET_V_W), act_dtype),
        jax.ShapeDtypeStruct((m, RET_V_W), F32),
        jax.ShapeDtypeStruct((m, FOX_W), act_dtype),
        jax.ShapeDtypeStruct((m, FOX_W), F32),
        jax.ShapeDtypeStruct((m, FOX_W), F32),
        jax.ShapeDtypeStruct((m, D_MODEL), F32),
        jax.ShapeDtypeStruct((m, D_MODEL), F32),
        jax.ShapeDtypeStruct((m, FOX_HEADS), F32),
        jax.ShapeDtypeStruct((m, LANES), F32),
    ]
    out_specs = [row(RET_QK_W), row(RET_QK_W), row(RET_V_W), row(RET_V_W), row(FOX_W),
                 row(FOX_W), row(FOX_W), row(D_MODEL), row(D_MODEL), row(FOX_HEADS), row(LANES)]
    if prompt_seq is not None:
        nt = prompt_seq // tm
        out_shape += [
            jax.ShapeDtypeStruct((m, FOX_W), BF16),
            jax.ShapeDtypeStruct((m // prompt_seq, FOX_HEADS, nt, FOX_DH, tm), BF16),
        ]
        out_specs += [row(FOX_W),
                      pl.BlockSpec((1, FOX_HEADS, 1, FOX_DH, tm), lambda i: (i // nt, 0, i % nt, 0, 0))]
    return pl.pallas_call(
        _inproj_kernel,
        grid=(m // tm,),
        in_specs=[row(D_MODEL), _resident(w_main.shape), _resident(w_gates.shape),
                  _resident(w_ff.shape), tab, tab, _resident(bf_pad.shape)],
        out_specs=out_specs,
        out_shape=out_shape,
        compiler_params=pltpu.CompilerParams(
            dimension_semantics=("parallel",), vmem_limit_bytes=VMEM_LIMIT_LARGE),
        name="inproj",
    )(x, w_main, w_gates, w_ff, cos_t, sin_t, bf_pad)


def _cumsum_kernel(lf_ref, fcol_ref, frow_ref, *, blk):
    t = lf_ref.shape[0]
    ri = lax.broadcasted_iota(jnp.int32, (blk, blk), 0)
    ci = lax.broadcasted_iota(jnp.int32, (blk, blk), 1)
    tri = (ri >= ci).astype(F32)
    carry = jnp.zeros((1, LANES), F32)
    for b in range(t // blk):
        xb = lf_ref[b * blk:(b + 1) * blk, :]
        fb = jnp.dot(tri, xb, precision=HIGHEST, preferred_element_type=F32) + carry
        fcol_ref[b * blk:(b + 1) * blk, :] = fb
        frow_ref[:, b * blk:(b + 1) * blk] = fb.T[:FOX_HEADS, :]
        carry = fb[blk - 1:blk, :]


def _cumsum(lf_pad, *, batch, seq):
    return pl.pallas_call(
        functools.partial(_cumsum_kernel, blk=LANES),
        grid=(batch,),
        in_specs=[pl.BlockSpec((seq, LANES), lambda b: (b, 0))],
        out_specs=(pl.BlockSpec((seq, LANES), lambda b: (b, 0)),
                   pl.BlockSpec((FOX_HEADS, seq), lambda b: (b, 0))),
        out_shape=(jax.ShapeDtypeStruct((batch * seq, LANES), F32),
                   jax.ShapeDtypeStruct((batch * FOX_HEADS, seq), F32)),
        compiler_params=pltpu.CompilerParams(dimension_semantics=("parallel",)),
        name="logf_cumsum",
    )(lf_pad)


def _ret_head_out(o, gain, rg):
    mu = jnp.mean(o, axis=-1, keepdims=True)
    oc = o - mu
    var = jnp.mean(oc * oc, axis=-1, keepdims=True)
    y = oc * lax.rsqrt(var + GN_EPS) * gain
    return y * (rg * jax.nn.sigmoid(rg))


def _ret_prompt_kernel(q_ref, k_ref, v_ref, rg_ref, gain_ref, after_ref, o_ref, sfin_ref, s_scr,
                       *, chunk):
    del after_ref
    c = pl.program_id(1)

    @pl.when(c == 0)
    def _():
        s_scr[...] = jnp.zeros_like(s_scr)

    ii = lax.broadcasted_iota(jnp.int32, (chunk, chunk), 0)
    jj = lax.broadcasted_iota(jnp.int32, (chunk, chunk), 1)
    causal = ii >= jj
    expo = jnp.where(causal, (ii - jj).astype(F32), 0.0)
    ic = lax.broadcasted_iota(jnp.int32, (chunk, 1), 0).astype(F32)
    for h in range(RET_HEADS):
        ld = _ret_log_decay(h)
        dmat = jnp.where(causal, jnp.exp(expo * ld), 0.0)
        cross_decay = jnp.exp((ic + 1.0) * ld)
        k_decay = jnp.exp((chunk - 1.0 - ic) * ld)
        chunk_decay = math.exp(chunk * ld)
        s_run = s_scr[h]
        for cc in range(q_ref.shape[0] // chunk):
            rows = slice(cc * chunk, (cc + 1) * chunk)
            qh = q_ref[rows, h * RET_DK:(h + 1) * RET_DK].astype(BF16)
            kh = k_ref[rows, h * RET_DK:(h + 1) * RET_DK]
            vh = v_ref[rows, h * RET_DV:(h + 1) * RET_DV].astype(BF16)
            sc = lax.dot_general(qh, kh.astype(BF16), NT_DIMS, preferred_element_type=F32) * dmat
            intra = jnp.dot(sc.astype(BF16), vh, preferred_element_type=F32)
            cross = jnp.dot(qh, s_run.astype(BF16), preferred_element_type=F32) * cross_decay
            kd = (kh * k_decay).astype(BF16)
            s_run = s_run * chunk_decay + lax.dot_general(kd, vh, TN_DIMS, preferred_element_type=F32)
            sl = slice(h * RET_DV, (h + 1) * RET_DV)
            o_ref[rows, sl] = _ret_head_out(
                intra + cross, gain_ref[:, sl], rg_ref[rows, sl]).astype(o_ref.dtype)
        s_scr[h] = s_run

    @pl.when(c == pl.num_programs(1) - 1)
    def _():
        sfin_ref[0] = s_scr[...]


def _ret_prompt(rq, rk, rv, rg, gain, after, *, batch, seq, chunks_per_step):
    rows = CHUNK * chunks_per_step
    nc = seq // rows
    row = lambda w: pl.BlockSpec((rows, w), lambda b, c: (b * nc + c, 0))
    return pl.pallas_call(
        functools.partial(_ret_prompt_kernel, chunk=CHUNK),
        grid=(batch, nc),
        in_specs=[row(RET_QK_W), row(RET_QK_W), row(RET_V_W), row(RET_V_W),
                  pl.BlockSpec((1, RET_V_W), lambda b, c: (0, 0)),
                  pl.BlockSpec(memory_space=pl.ANY)],
        out_specs=(row(RET_V_W),
                   pl.BlockSpec((1, RET_HEADS, RET_DK, RET_DV), lambda b, c: (b, 0, 0, 0))),
        out_shape=(jax.ShapeDtypeStruct((batch * seq, RET_V_W), BF16),
                   jax.ShapeDtypeStruct((batch, RET_HEADS, RET_DK, RET_DV), F32)),
        scratch_shapes=[pltpu.VMEM((RET_HEADS, RET_DK, RET_DV), F32)],
        compiler_params=pltpu.CompilerParams(
            dimension_semantics=("parallel", "arbitrary"), vmem_limit_bytes=VMEM_LIMIT_MEDIUM),
        name="ret_prompt",
    )(rq, rk, rv, rg, gain, after)


def _ret_sample_kernel(q_ref, k_ref, v_ref, rg_ref, gain_ref, s0_ref, o_ref, s1_ref, *, group, t_new):
    rows = group * t_new
    shift = t_new.bit_length() - 1
    ri = lax.broadcasted_iota(jnp.int32, (rows, rows), 0)
    ci = lax.broadcasted_iota(jnp.int32, (rows, rows), 1)
    valid = ((ri >> shift) == (ci >> shift)) & (ri >= ci)
    expo = jnp.where(valid, (ri - ci).astype(F32), 0.0)
    r1 = lax.broadcasted_iota(jnp.int32, (rows, 1), 0)
    tpos = (r1 & (t_new - 1)).astype(F32)
    rowgrp = lax.broadcasted_iota(jnp.int32, (rows, RET_DK), 0) >> shift
    for h in range(RET_HEADS):
        ld = _ret_log_decay(h)
        dmat = jnp.where(valid, jnp.exp(expo * ld), 0.0)
        cross_decay = jnp.exp((tpos + 1.0) * ld)
        k_decay = jnp.exp((t_new - 1.0 - tpos) * ld)
        chunk_decay = math.exp(t_new * ld)
        qh = q_ref[:, h * RET_DK:(h + 1) * RET_DK]
        kh = k_ref[:, h * RET_DK:(h + 1) * RET_DK]
        vh = v_ref[:, h * RET_DV:(h + 1) * RET_DV].astype(BF16)
        qb = qh.astype(BF16)
        sc = lax.dot_general(qb, kh.astype(BF16), NT_DIMS, preferred_element_type=F32) * dmat
        o = jnp.dot(sc.astype(BF16), vh, preferred_element_type=F32)
        kd = kh * k_decay
        cross = jnp.zeros((rows, RET_DV), F32)
        for g in range(group):
            s_old = s0_ref[g, h]
            in_g = rowgrp == g
            qg = jnp.where(in_g, qh, 0.0).astype(BF16)
            cross = cross + jnp.dot(qg, s_old.astype(BF16), preferred_element_type=F32)
            kg = jnp.where(in_g, kd, 0.0).astype(BF16)
            s1_ref[g, h] = s_old * chunk_decay + lax.dot_general(
                kg, vh, TN_DIMS, preferred_element_type=F32)
        o = o + cross * cross_decay
        sl = slice(h * RET_DV, (h + 1) * RET_DV)
        o_ref[:, sl] = _ret_head_out(o, gain_ref[:, sl], rg_ref[:, sl]).astype(o_ref.dtype)


def _ret_sample(rq, rk, rv, rg, gain, state, *, t_new, group):
    m = rq.shape[0]
    rows = group * t_new
    row = lambda w: pl.BlockSpec((rows, w), lambda i: (i, 0))
    st = pl.BlockSpec((group, RET_HEADS, RET_DK, RET_DV), lambda i: (i, 0, 0, 0))
    return pl.pallas_call(
        functools.partial(_ret_sample_kernel, group=group, t_new=t_new),
        grid=(m // rows,),
        in_specs=[row(RET_QK_W), row(RET_QK_W), row(RET_V_W), row(RET_V_W),
                  pl.BlockSpec((1, RET_V_W), lambda i: (0, 0)), st],
        out_specs=(row(RET_V_W), st),
        out_shape=(jax.ShapeDtypeStruct((m, RET_V_W), BF16),
                   jax.ShapeDtypeStruct(state.shape, F32)),
        compiler_params=pltpu.CompilerParams(
            dimension_semantics=("parallel",), vmem_limit_bytes=VMEM_LIMIT_MEDIUM),
        name="ret_sample",
    )(rq, rk, rv, rg, gain, state)


def _fox_prompt_kernel(q_ref, k_ref, vt_ref, fcum_ref, frow_ref, o_ref,
                       fk_scr, t_scr, p_scr, acc_scr, *, tile, scale):
    hps = vt_ref.shape[1]
    h0 = pl.program_id(1) * hps
    i = pl.program_id(2)
    seq = k_ref.shape[0]
    vt_tile = vt_ref.shape[-1]
    per = tile // vt_tile
    assert per % 2 == 0

    @pl.when(i == 0)
    def _():
        lane = lax.broadcasted_iota(jnp.int32, (seq, LANES), 1)
        for g in range(hps):
            col = jnp.sum(jnp.where(lane == h0 + g, fcum_ref[...], 0.0), axis=1, keepdims=True)
            fk_scr[g] = jnp.broadcast_to(col * LOG2E, (seq, LANES))

    def scores(g, u):
        start = pl.multiple_of(u * vt_tile, vt_tile)
        ks = k_ref[pl.ds(start, vt_tile), g * FOX_DH:(g + 1) * FOX_DH]
        fk2 = fk_scr[g, pl.ds(start, vt_tile), :]
        t = lax.dot_general(ks, q_ref[:, g * FOX_DH:(g + 1) * FOX_DH], NT_DIMS,
                            preferred_element_type=F32) * (scale * LOG2E)
        return t - jnp.concatenate([fk2] * (tile // LANES), axis=1)

    def sub_step(g, u, slot, state, *, diag_part=None, last=False):
        m_run, l_run, alpha_prev = state
        fq2 = frow_ref[g, pl.ds(i, 1), :] * LOG2E
        if not last:
            t_scr[g, 1 - slot] = scores(g, u + 1)
        pv_prev = jnp.dot(vt_ref[0, g, jnp.maximum(u - 1, 0)], p_scr[g, 1 - slot],
                          preferred_element_type=F32)
        acc_scr[g] = alpha_prev * acc_scr[g] + pv_prev
        t = t_scr[g, slot]
        if diag_part is not None:
            key = diag_part * vt_tile + lax.broadcasted_iota(jnp.int32, (vt_tile, tile), 0)
            qry = lax.broadcasted_iota(jnp.int32, (vt_tile, tile), 1)
            t = jnp.where(key <= qry, t, -jnp.inf)
        m_new = jnp.maximum(m_run, jnp.max(t, axis=0, keepdims=True) + fq2)
        alpha = jnp.exp2(m_run - m_new)
        p = jnp.exp2(t + (fq2 - m_new))
        l_new = alpha * l_run + jnp.sum(p, axis=0, keepdims=True)
        p_scr[g, slot] = p.astype(BF16)
        return m_new, l_new, alpha

    def pair(j, states):
        for s in range(per):
            states = tuple(sub_step(g, j * per + s, s % 2, states[g]) for g in range(hps))
        return states

    for g in range(hps):
        t_scr[g, 0] = scores(g, 0)
        p_scr[g, 1] = jnp.zeros((vt_tile, tile), BF16)
        acc_scr[g] = jnp.zeros((FOX_DH, tile), F32)
    init = (jnp.full((1, tile), -jnp.inf, F32), jnp.zeros((1, tile), F32), jnp.ones((1, tile), F32))
    states = lax.fori_loop(0, i, pair, (init,) * hps)
    for s in range(per):
        states = tuple(sub_step(g, i * per + s, s % 2, states[g], diag_part=s, last=(s == per - 1))
                       for g in range(hps))
    last_u = i * per + per - 1
    for g in range(hps):
        _, l_fin, alpha_prev = states[g]
        acc = alpha_prev * acc_scr[g] + jnp.dot(vt_ref[0, g, last_u], p_scr[g, (per - 1) % 2],
                                         preferred_element_type=F32)
        o_ref[:, g * FOX_DH:(g + 1) * FOX_DH] = (acc / l_fin).T.astype(o_ref.dtype)


def _fox_prompt(fq, fkb, fvt, fcum, frow, *, batch, seq, tile, heads_per_step):
    nq = seq // tile
    hps = heads_per_step
    frow3 = frow.reshape(batch * FOX_HEADS, nq, tile)
    nt, vt_tile = fvt.shape[2], fvt.shape[4]
    return pl.pallas_call(
        functools.partial(_fox_prompt_kernel, tile=tile, scale=FOX_DH ** -0.5),
        grid=(batch, FOX_HEADS // hps, nq),
        in_specs=[pl.BlockSpec((tile, hps * FOX_DH), lambda b, h, i: (b * nq + i, h)),
                  pl.BlockSpec((seq, hps * FOX_DH), lambda b, h, i: (b, h)),
                  pl.BlockSpec((1, hps, nt, FOX_DH, vt_tile), lambda b, h, i: (b, h, 0, 0, 0)),
                  pl.BlockSpec((seq, LANES), lambda b, h, i: (b, 0)),
                  pl.BlockSpec((hps, nq, tile), lambda b, h, i: (b * (FOX_HEADS // hps) + h, 0, 0))],
        out_specs=pl.BlockSpec((tile, hps * FOX_DH), lambda b, h, i: (b * nq + i, h)),
        out_shape=jax.ShapeDtypeStruct((batch * seq, FOX_W), BF16),
        scratch_shapes=[pltpu.VMEM((hps, seq, LANES), F32),
                        pltpu.VMEM((hps, 2, vt_tile, tile), F32),
                        pltpu.VMEM((hps, 2, vt_tile, tile), BF16),
                        pltpu.VMEM((hps, FOX_DH, tile), F32)],
        compiler_params=pltpu.CompilerParams(
            dimension_semantics=("parallel", "parallel", "arbitrary")),
        name="fox_prompt",
    )(fq, fkb, fvt, fcum, frow3)


def _fox_sample_kernel(pt_ref, q_ref, kn_ref, vn_ref, ln_ref, *refs, n_pages, scale, page_cols):
    del pt_ref
    kc_refs = refs[:n_pages]
    vc_refs = refs[n_pages:2 * n_pages]
    lc_refs = refs[2 * n_pages:3 * n_pages]
    o_ref, s_scr, g_scr = refs[3 * n_pages:]
    t_new = q_ref.shape[0]
    assert t_new == SUBLANES
    n_rows = FOX_HEADS * t_new
    tshift = t_new.bit_length() - 1
    hmask = FOX_HEADS - 1
    li = lax.broadcasted_iota(jnp.int32, (LANES, LANES), 0)
    lj = lax.broadcasted_iota(jnp.int32, (LANES, LANES), 1)
    same_head = ((li - lj) & hmask) == 0

    def stack_heads(ref):
        return jnp.concatenate([ref[:, h * FOX_DH:(h + 1) * FOX_DH] for h in range(FOX_HEADS)], axis=0)

    prefix = (((li >> tshift) == (lj >> tshift)) & (li <= lj)).astype(F32)
    xn = jnp.broadcast_to(ln_ref[0], (SUBLANES, LANES))
    fn_row = jnp.dot(xn, prefix, precision=HIGHEST, preferred_element_type=F32)[0:1, :]
    eye = (lax.broadcasted_iota(jnp.int32, (n_rows, LANES), 0)
           == lax.broadcasted_iota(jnp.int32, (n_rows, LANES), 1))
    fq = jnp.sum(jnp.where(eye, fn_row, 0.0), axis=1, keepdims=True)

    assert n_pages * FOX_HEADS == LANES and lc_refs[0].shape[1:] == (FOX_HEADS, LANES)
    x = jnp.concatenate([lc_refs[pg][0] for pg in range(n_pages)], axis=0)
    within = jnp.dot(x, (li > lj).astype(F32), precision=HIGHEST, preferred_element_type=F32)
    tot = jnp.broadcast_to(jnp.sum(x, axis=1, keepdims=True), (LANES, LANES))
    later_pages = jnp.dot((same_head & (lj > li)).astype(F32), tot, precision=HIGHEST,
                          preferred_element_type=F32)
    g2 = (within + later_pages) * (-LOG2E)
    hshift = FOX_HEADS.bit_length() - 1
    spread = ((lax.broadcasted_iota(jnp.int32, (LANES, page_cols), 1) >> hshift)
              == lax.broadcasted_iota(jnp.int32, (LANES, page_cols), 0)).astype(BF16)
    g_hi = g2.astype(BF16)
    rest = g2 - g_hi.astype(F32)
    g_mid = rest.astype(BF16)
    g_lo = (rest - g_mid.astype(F32)).astype(BF16)
    g_cols = (jnp.dot(g_hi, spread, preferred_element_type=F32)
              + jnp.dot(g_mid, spread, preferred_element_type=F32)
              + jnp.dot(g_lo, spread, preferred_element_type=F32))
    own_head = ((lax.broadcasted_iota(jnp.int32, (LANES, page_cols), 1)
                 - lax.broadcasted_iota(jnp.int32, (LANES, page_cols), 0)) & hmask) == 0
    g_scr[...] = jnp.where(own_head, g_cols, jnp.inf)

    def page_bias(pg):
        return jnp.concatenate(
            [jnp.broadcast_to(g_scr[pg * FOX_HEADS + h:pg * FOX_HEADS + h + 1, :], (t_new, page_cols))
             for h in range(FOX_HEADS)], axis=0)

    c_qk = scale * LOG2E
    fq2 = fq * LOG2E
    qb = stack_heads(q_ref).astype(BF16)
    mx = jnp.full((n_rows, 1), -jnp.inf, F32)
    for pg in range(n_pages):
        s = lax.dot_general(qb, kc_refs[pg][0].astype(BF16), NT_DIMS, preferred_element_type=F32)
        s = s * c_qk - page_bias(pg)
        s_scr[pg] = s
        mx = jnp.maximum(mx, jnp.max(s, axis=1, keepdims=True))

    zpad = jnp.zeros((LANES - n_rows, FOX_DH), BF16)
    kn = jnp.concatenate([stack_heads(kn_ref).astype(BF16), zpad], axis=0)
    vn = jnp.concatenate([stack_heads(vn_ref).astype(BF16), zpad], axis=0)
    r2 = lax.broadcasted_iota(jnp.int32, (n_rows, LANES), 0)
    c2 = lax.broadcasted_iota(jnp.int32, (n_rows, LANES), 1)
    ok = ((r2 >> tshift) == (c2 >> tshift)) & (c2 <= r2)
    s2 = lax.dot_general(qb, kn, NT_DIMS, preferred_element_type=F32)
    s2 = jnp.where(ok, s2 * c_qk - fn_row * LOG2E, -jnp.inf)
    mx = jnp.maximum(mx, jnp.max(s2, axis=1, keepdims=True))

    shift = fq2 - (mx + fq2)
    l = jnp.zeros((n_rows, 1), F32)
    acc = jnp.zeros((n_rows, FOX_DH), F32)
    for pg in range(n_pages):
        pr = jnp.exp2(s_scr[pg] + shift)
        l = l + jnp.sum(pr, axis=1, keepdims=True)
        acc = acc + jnp.dot(pr.astype(BF16), vc_refs[pg][0].astype(BF16), preferred_element_type=F32)
    p2 = jnp.exp2(s2 + shift)
    l = l + jnp.sum(p2, axis=1, keepdims=True)
    acc = acc + jnp.dot(p2.astype(BF16), vn, preferred_element_type=F32)
    out = acc / l
    for h in range(FOX_HEADS):
        o_ref[:, h * FOX_DH:(h + 1) * FOX_DH] = out[h * t_new:(h + 1) * t_new, :]


def _page_index(b, pt, *, j):
    return (pt[b, j], 0, 0)


def _fox_sample(page_table, fq, fk, fv, lf_new, cache_k, cache_v, cache_lf, *, t_new):
    dec_batch, n_pages = page_table.shape
    n_phys, page_size = cache_k.shape[0], cache_k.shape[1]
    n_rows = t_new * FOX_HEADS
    page_cols = page_size * FOX_HEADS
    ln = lf_new.reshape(dec_batch, t_new, FOX_HEADS).transpose(0, 2, 1).reshape(dec_batch, 1, n_rows)
    ln = jnp.pad(ln, ((0, 0), (0, 0), (0, LANES - n_rows)))
    kc = cache_k.reshape(n_phys, page_cols, FOX_DH)
    vc = cache_v.reshape(n_phys, page_cols, FOX_DH)
    lc = jnp.swapaxes(cache_lf, 1, 2)
    new_spec = pl.BlockSpec((t_new, FOX_W), lambda b, pt: (b, 0))
    kv_specs = [pl.BlockSpec((1, page_cols, FOX_DH), functools.partial(_page_index, j=j))
                for j in range(n_pages)]
    lf_specs = [pl.BlockSpec((1, FOX_HEADS, page_size), functools.partial(_page_index, j=j))
                for j in range(n_pages)]
    grid_spec = pltpu.PrefetchScalarGridSpec(
        num_scalar_prefetch=1,
        grid=(dec_batch,),
        in_specs=[new_spec, new_spec, new_spec, pl.BlockSpec((1, 1, LANES), lambda b, pt: (b, 0, 0))]
        + kv_specs + kv_specs + lf_specs,
        out_specs=new_spec,
        scratch_shapes=[pltpu.VMEM((n_pages, n_rows, page_cols), F32),
                        pltpu.VMEM((n_pages * FOX_HEADS, page_cols), F32)],
    )
    return pl.pallas_call(
        functools.partial(_fox_sample_kernel, n_pages=n_pages, scale=FOX_DH ** -0.5,
                          page_cols=page_cols),
        grid_spec=grid_spec,
        out_shape=jax.ShapeDtypeStruct((dec_batch * t_new, FOX_W), F32),
        compiler_params=pltpu.CompilerParams(
            dimension_semantics=("parallel",), vmem_limit_bytes=VMEM_LIMIT_LARGE),
        name="fox_sample",
    )(page_table, fq, fk, fv, ln, *([kc] * n_pages), *([vc] * n_pages), *([lc] * n_pages))


def _merge_kernel(ro_ref, fo_ref, ga_ref, gb_ref, x_ref, wr_ref, wf_ref, wo_ref, g_ref, b_ref, h_ref):
    a = jnp.dot(ro_ref[...].astype(BF16), wr_ref[...], preferred_element_type=F32)
    b = jnp.dot(fo_ref[...].astype(BF16), wf_ref[...], preferred_element_type=F32)
    merged = jax.nn.sigmoid(ga_ref[...]) * a + jax.nn.sigmoid(gb_ref[...]) * b
    mix = jnp.dot(merged.astype(BF16), wo_ref[...], preferred_element_type=F32)
    h_ref[...] = _layer_norm(ALPHA * x_ref[...] + mix, g_ref[...], b_ref[...])


def _merge(ret_o, fox_o, ga, gb, x, wr, wf, wo, g, b, *, tm):
    m = x.shape[0]
    row = lambda w: pl.BlockSpec((tm, w), lambda i: (i, 0))
    return pl.pallas_call(
        _merge_kernel,
        grid=(m // tm,),
        in_specs=[row(RET_V_W), row(FOX_W), row(D_MODEL), row(D_MODEL), row(D_MODEL),
                  _resident(wr.shape), _resident(wf.shape), _resident(wo.shape),
                  _resident(g.shape), _resident(b.shape)],
        out_specs=row(D_MODEL),
        out_shape=jax.ShapeDtypeStruct((m, D_MODEL), F32),
        compiler_params=pltpu.CompilerParams(
            dimension_semantics=("parallel",), vmem_limit_bytes=VMEM_LIMIT_MEDIUM),
        name="merge_out_ln",
    )(ret_o, fox_o, ga, gb, x, wr, wf, wo, g, b)


def _ffn_kernel(h_ref, wu_ref, wd_ref, g_ref, b_ref, y_ref, *, ff_tile):
    h = h_ref[...]
    hb = h.astype(BF16)
    acc = jnp.zeros(h.shape, F32)
    for c in range(D_FF // ff_tile):
        sl = slice(c * ff_tile, (c + 1) * ff_tile)
        u = jnp.maximum(jnp.dot(hb, wu_ref[:, sl], preferred_element_type=F32), 0.0)
        acc = acc + jnp.dot((u * u).astype(BF16), wd_ref[sl, :], preferred_element_type=F32)
    y_ref[...] = _layer_norm(ALPHA * h + acc, g_ref[...], b_ref[...])


def _ffn(h, wu, wd, g, b, *, tm, ff_tile):
    m = h.shape[0]
    row = pl.BlockSpec((tm, D_MODEL), lambda i: (i, 0))
    return pl.pallas_call(
        functools.partial(_ffn_kernel, ff_tile=ff_tile),
        grid=(m // tm,),
        in_specs=[row, _resident(wu.shape), _resident(wd.shape), _resident(g.shape),
                  _resident(b.shape)],
        out_specs=row,
        out_shape=jax.ShapeDtypeStruct((m, D_MODEL), F32),
        compiler_params=pltpu.CompilerParams(
            dimension_semantics=("parallel",), vmem_limit_bytes=VMEM_LIMIT_MEDIUM),
        name="ffn_ln",
    )(h, wu, wd, g, b)


def _rope_tables(pos):
    half = RET_DK // 2
    inv_freq = ROPE_BASE ** (-np.arange(half, dtype=np.float64) / half)
    ang = np.asarray(pos, np.float64)[:, None] * inv_freq[None, :]
    c, s = np.cos(ang), np.sin(ang)
    return (jnp.asarray(np.concatenate([c, c], axis=1), F32),
            jnp.asarray(np.concatenate([-s, s], axis=1), F32))


def kernel(x_prompt, x_sample, cache_k, cache_v, cache_logf, state_ret, page_table, w_in, b_forget,
           ret_gn_gain, w_ret_proj, w_fox_proj, w_out, ln1_g, ln1_b, w_ff_up, w_ff_down, ln2_g,
           ln2_b):
    batch, seq, _ = x_prompt.shape
    dec_batch, t_new, _ = x_sample.shape
    page_size = cache_k.shape[2]
    past_len = page_table.shape[1] * page_size
    tm = ROW_TILE
    assert seq % ATTN_QUERY_TILE == 0 and seq % (CHUNK * RET_CHUNKS_PER_STEP) == 0
    assert (batch * seq) % TAIL_ROW_TILE == 0 and (dec_batch * t_new) % TAIL_ROW_TILE == 0
    assert dec_batch % RET_SAMPLE_GROUP == 0 and tm % t_new == 0

    wt = jnp.swapaxes(w_in[0], 0, 1)
    w_pack = (wt[:C_FF].astype(BF16), wt[C_GATES:].astype(BF16),
              jnp.pad(wt[C_FF:C_GATES], ((0, LANES - FOX_HEADS), (0, 0))).astype(BF16))
    bf_pad = jnp.pad(b_forget[0].astype(F32), (0, LANES - FOX_HEADS)).reshape(1, LANES)
    gain = ret_gn_gain[0].reshape(1, RET_V_W)
    wr, wf, wo = w_ret_proj[0].astype(BF16), w_fox_proj[0].astype(BF16), w_out[0].astype(BF16)
    wu, wd = w_ff_up[0].astype(BF16), w_ff_down[0].astype(BF16)
    g1, b1 = ln1_g[0].reshape(1, D_MODEL), ln1_b[0].reshape(1, D_MODEL)
    g2, b2 = ln2_g[0].reshape(1, D_MODEL), ln2_b[0].reshape(1, D_MODEL)

    def tail(ret_o, fox_o, ga, gb, x2):
        h = _merge(ret_o, fox_o, ga, gb, x2, wr, wf, wo, g1, b1, tm=TAIL_ROW_TILE)
        return _ffn(h, wu, wd, g2, b2, tm=TAIL_ROW_TILE, ff_tile=D_FF)

    xp = x_prompt.reshape(batch * seq, D_MODEL)
    cos_p, sin_p = _rope_tables(np.arange(seq))
    (rq, rk, rv, rg, fq, fk, fv, ga, gb, lf8, lfp, fkb, fvt) = _inproj(
        xp, w_pack, cos_p, sin_p, bf_pad, tm=tm, act_dtype=BF16, prompt_seq=seq)
    fcum, frow = _cumsum(lfp, batch=batch, seq=seq)
    fox_o = _fox_prompt(fq, fkb, fvt, fcum, frow, batch=batch, seq=seq, tile=ATTN_QUERY_TILE,
                        heads_per_step=ATTN_HEADS_PER_STEP)
    ret_o, s_prompt = _ret_prompt(rq, rk, rv, rg, gain, fox_o, batch=batch, seq=seq,
                                  chunks_per_step=RET_CHUNKS_PER_STEP)
    y_prompt = tail(ret_o, fox_o, ga, gb, xp).reshape(batch, seq, D_MODEL)

    xs = x_sample.reshape(dec_batch * t_new, D_MODEL)
    cos_s, sin_s = _rope_tables(past_len + np.arange(tm) % t_new)
    (rq_s, rk_s, rv_s, rg_s, fq_s, fk_s, fv_s, ga_s, gb_s, lf8_s, _) = _inproj(
        xs, w_pack, cos_s, sin_s, bf_pad, tm=tm, act_dtype=F32)
    ret_o_s, s_sample = _ret_sample(rq_s, rk_s, rv_s, rg_s, gain, state_ret[0], t_new=t_new,
                                    group=RET_SAMPLE_GROUP)
    fox_o_s = _fox_sample(page_table, fq_s, fk_s, fv_s, lf8_s, cache_k[0], cache_v[0], cache_logf[0],
                          t_new=t_new)
    y_sample = tail(ret_o_s, fox_o_s, ga_s, gb_s, xs).reshape(dec_batch, t_new, D_MODEL)

    return (y_prompt, y_sample,
            fk.reshape(1, batch, seq, FOX_HEADS, FOX_DH), fv.reshape(1, batch, seq, FOX_HEADS, FOX_DH),
            lf8.reshape(1, batch, seq, FOX_HEADS), s_prompt[None],
            fk_s.reshape(1, dec_batch, t_new, FOX_HEADS, FOX_DH),
            fv_s.reshape(1, dec_batch, t_new, FOX_HEADS, FOX_DH),
            lf8_s.reshape(1, dec_batch, t_new, FOX_HEADS), s_sample[None])
```
